```python
import math
import jax, jax.numpy as jnp
from jax import lax
import numpy as np

D_MODEL = 2048
BATCH = 4
SEQ = 2048
DEPTH = 4
DEC_BATCH = 8
DEC_SEQ = 1
PAST_LEN = 16384
PAGE_SIZE = 128

F32 = jnp.float32
EPS = 1e-6
N_MIXERS = 3
POOL_WINDOWS = (2, 4, 8, 16)
POOL_GROUP = D_MODEL // len(POOL_WINDOWS)
POOL_HIST = max(POOL_WINDOWS) - 1
DIL_PATTERNS = ((128, 1), (512, 4), (2048, 16))
DIL_GROUPS = len(DIL_PATTERNS)
DIL_HEADS = 8
DIL_HEAD_DIM = 128
DIL_WIDTH = DIL_GROUPS * DIL_HEADS * DIL_HEAD_DIM
DIL_BLOCK = 128
RET_HEADS = 8
RET_DK = D_MODEL // RET_HEADS
RET_DV = 2 * RET_DK
RET_QK = RET_HEADS * RET_DK
RET_V = RET_HEADS * RET_DV
RET_CHUNK = 128
PEER_HEADS = 8
PEER_NKEYS = 128
PEER_EXPERTS = PEER_NKEYS * PEER_NKEYS
PEER_KEY_DIM = 256
PEER_HALF = PEER_KEY_DIM // 2
PEER_TOPK = 16
PEER_BLOCK = 64
N_POOL_LAYERS = len(range(0, DEPTH, N_MIXERS))
N_DIL_LAYERS = len(range(1, DEPTH, N_MIXERS))
N_RET_LAYERS = len(range(2, DEPTH, N_MIXERS))

kernel_name = 'hybrid_pool_dilated_retention_peer_adaln_step'


def rmsnorm(x, g):
    xf = x.astype(F32)
    y = xf * lax.rsqrt(jnp.mean(xf * xf, axis=-1, keepdims=True) + EPS)
    return (y * g.astype(F32)).astype(x.dtype)


def adaln(c, w, b):
    m = (jax.nn.silu(c) @ w + b)[:, None, :]
    return jnp.split(m, 6, axis=-1)


def modulate(h, shift, scale):
    return h * (1.0 + scale) + shift


def alibi_slopes():
    n = DIL_GROUPS * DIL_HEADS
    return jnp.exp2(-8.0 * jnp.arange(1, n + 1, dtype=F32) / n).reshape(DIL_GROUPS, DIL_HEADS)


def pool_mixer(h, hist, pos0, w_pool, scale):
    n, l, _ = h.shape
    ext = h if hist is None else jnp.concatenate([hist.astype(h.dtype), h], axis=1)
    p = ext.shape[1] - l
    ef = ext.astype(F32)
    off = POOL_HIST + 1
    cs = jnp.concatenate([jnp.zeros((n, off, D_MODEL), F32), jnp.cumsum(ef, axis=1)], axis=1)
    pos = pos0 + jnp.arange(l)
    outs = []
    for gi, wdw in enumerate(POOL_WINDOWS):
        sl = slice(gi * POOL_GROUP, (gi + 1) * POOL_GROUP)
        win_sum = cs[:, off + p:off + p + l, sl] - cs[:, off + p - wdw:off + p - wdw + l, sl]
        cnt = jnp.minimum(wdw, pos + 1).astype(F32)[None, :, None]
        outs.append(win_sum / cnt - ef[:, p:, sl])
    d = jnp.stack(outs, axis=2)
    y = jnp.einsum('nlgc,gce->nlge', d, w_pool.astype(F32)).reshape(n, l, D_MODEL)
    y = y * scale.astype(F32)
    return y.astype(h.dtype), ext[:, -POOL_HIST:]


def band_attention(q, k, v, n_back, dil, slopes):
    n, ls, hh, hd = q.shape
    blk = DIL_BLOCK
    nb = -(-ls // blk)
    pad = nb * blk - ls

    def blocks(a):
        return jnp.pad(a, ((0, 0), (0, pad), (0, 0), (0, 0))).reshape(n, nb, blk, hh, hd)

    def with_prev(a):
        prev = jnp.pad(a, ((0, 0), (1, 0), (0, 0), (0, 0), (0, 0)))[:, :-1]
        return jnp.concatenate([prev, a], axis=2)

    qb = blocks(q)
    kk, vv = with_prev(blocks(k)), with_prev(blocks(v))
    s = jnp.einsum('nbqhd,nbkhd->nbhqk', qb, kk, preferred_element_type=F32) * (hd ** -0.5)
    qpos = jnp.arange(nb)[:, None] * blk + jnp.arange(blk)[None, :]
    kpos = jnp.arange(nb)[:, None] * blk - blk + jnp.arange(2 * blk)[None, :]
    dist = qpos[:, :, None] - kpos[:, None, :]
    valid = (dist >= 0) & (dist <= n_back) & (kpos[:, None, :] >= 0)
    s = s - slopes[None, :, None, None] * (dil * dist).astype(F32)[:, None]
    s = jnp.where(valid[None, :, None], s, -jnp.inf)
    m = jnp.max(s, axis=-1, keepdims=True)
    pr = jnp.exp(s - m)
    den = jnp.sum(pr, axis=-1)
    o = jnp.einsum('nbhqk,nbkhd->nbqhd', pr, vv.astype(F32)) / jnp.swapaxes(den, 2, 3)[..., None]
    lse = jnp.swapaxes(m[..., 0] + jnp.log(den), 2, 3)
    return o.reshape(n, nb * blk, hh, hd)[:, :ls], lse.reshape(n, nb * blk, hh)[:, :ls]


def dil_group_prompt(q, k, v, wdw, dil, slopes):
    n, t, hh, hd = q.shape
    ls = t // dil

    def sub(a):
        return a.reshape(n, ls, dil, hh, hd).transpose(0, 2, 1, 3, 4).reshape(n * dil, ls, hh, hd)

    o, lse = band_attention(sub(q), sub(k), sub(v), wdw // dil, dil, slopes)
    o = o.reshape(n, dil, ls, hh, hd).transpose(0, 2, 1, 3, 4).reshape(n, t, hh, hd)
    lse = lse.reshape(n, dil, ls, hh).transpose(0, 2, 1, 3).reshape(n, t, hh)
    nl = min(wdw, t)
    kv = jnp.stack([k[:, t - nl:], v[:, t - nl:]], axis=2)
    return o, lse, kv


def dil_group_sample(q, k, v, buf, wdw, dil, slopes):
    n, l, hh, hd = q.shape
    lb = buf.shape[1]
    ke = jnp.concatenate([buf[:, :, 0].astype(k.dtype), k], axis=1)
    ve = jnp.concatenate([buf[:, :, 1].astype(v.dtype), v], axis=1)
    steps = jnp.arange(wdw // dil + 1)
    idx = lb + jnp.arange(l)[:, None] - dil * steps[None, :]
    valid = idx >= 0
    idx = jnp.maximum(idx, 0)
    kg, vg = ke[:, idx], ve[:, idx]
    s = jnp.einsum('blhd,blkhd->bhlk', q, kg, preferred_element_type=F32) * (hd ** -0.5)
    s = s - slopes[None, :, None, None] * (dil * steps).astype(F32)
    s = jnp.where(valid[None, None], s, -jnp.inf)
    m = jnp.max(s, axis=-1, keepdims=True)
    pr = jnp.exp(s - m)
    den = jnp.sum(pr, axis=-1)
    o = jnp.einsum('bhlk,blkhd->blhd', pr, vg.astype(F32)) / jnp.swapaxes(den, 1, 2)[..., None]
    lse = jnp.swapaxes(m[..., 0] + jnp.log(den), 1, 2)
    nl = min(wdw, lb + l)
    kv = jnp.stack([ke[:, -nl:], ve[:, -nl:]], axis=2)
    return o, lse, kv


def dilated_mixer(h, bufs, w_in, w_out):
    n, l, _ = h.shape
    qkv = (h @ w_in).reshape(n, l, 3, DIL_GROUPS, DIL_HEADS, DIL_HEAD_DIM)
    q, k, v = qkv[:, :, 0], qkv[:, :, 1], qkv[:, :, 2]
    slopes = alibi_slopes()
    outs, lses, new = [], [], []
    for g, (wdw, dil) in enumerate(DIL_PATTERNS):
        if bufs is None:
            o, lse, kv = dil_group_prompt(q[:, :, g], k[:, :, g], v[:, :, g], wdw, dil, slopes[g])
        else:
            o, lse, kv = dil_group_sample(q[:, :, g], k[:, :, g], v[:, :, g], bufs[g], wdw, dil, slopes[g])
        outs.append(o)
        lses.append(lse)
        new.append(kv)
    wts = jax.nn.softmax(jnp.stack(lses), axis=0)
    o = jnp.sum(wts[..., None] * jnp.stack(outs), axis=0)
    y = o.reshape(n, l, DIL_HEADS * DIL_HEAD_DIM).astype(h.dtype) @ w_out
    return y, new


def retention_scan(q, k, v, s0):
    n, l, hh, _ = q.shape
    dv = v.shape[-1]
    c = math.gcd(l, RET_CHUNK)
    nc = l // c
    log_g = jnp.log1p(-jnp.exp2(-5.0 - jnp.arange(hh, dtype=F32)))
    pos = jnp.arange(c, dtype=F32)
    diff = pos[:, None] - pos[None, :]
    dmask = jnp.where(diff >= 0, jnp.exp(log_g[:, None, None] * jnp.maximum(diff, 0.0)), 0.0)
    q_dec = jnp.exp(log_g[None, :] * (pos[:, None] + 1.0))
    k_dec = jnp.exp(log_g[None, :] * (c - 1.0 - pos[:, None]))
    c_dec = jnp.exp(log_g * c)

    def chunks(a):
        return jnp.swapaxes(a.reshape(n, nc, c, hh, a.shape[-1]), 0, 1)

    def step(state, xs):
        qc, kc, vc = xs
        att = jnp.einsum('nchd,nmhd->nhcm', qc, kc) * dmask
        o = (jnp.einsum('nhcm,nmhe->nche', att, vc)
             + jnp.einsum('nchd,nhde->nche', qc * q_dec[None, :, :, None], state))
        state = (c_dec[None, :, None, None] * state
                 + jnp.einsum('nchd,nche->nhde', kc * k_dec[None, :, :, None], vc))
        return state, o

    state, o = lax.scan(step, s0, (chunks(q), chunks(k), chunks(v)))
    return jnp.swapaxes(o, 0, 1).reshape(n, l, hh, dv), state


def retention_mixer(h, s0, w_in, gn_g, w_out):
    n, l, _ = h.shape
    proj = h @ w_in
    q, k, v, gate = jnp.split(proj, [RET_QK, 2 * RET_QK, 2 * RET_QK + RET_V], axis=-1)
    q = q.reshape(n, l, RET_HEADS, RET_DK).astype(F32)
    k = k.reshape(n, l, RET_HEADS, RET_DK).astype(F32) * (RET_DK ** -0.5)
    v = v.reshape(n, l, RET_HEADS, RET_DV).astype(F32)
    if s0 is None:
        s0 = jnp.zeros((n, RET_HEADS, RET_DK, RET_DV), F32)
    o, state = retention_scan(q, k, v, s0.astype(F32))
    mu = jnp.mean(o, axis=-1, keepdims=True)
    var = jnp.mean(jnp.square(o - mu), axis=-1, keepdims=True)
    o = ((o - mu) * lax.rsqrt(var + EPS)).reshape(n, l, RET_V) * gn_g.astype(F32)
    y = (jax.nn.silu(gate.astype(F32)) * o).astype(h.dtype) @ w_out
    return y, state.astype(h.dtype)


def peer(h, w_q, sub_keys, u_tab, v_tab):
    n, l, d = h.shape
    t = n * l
    blk = min(PEER_BLOCK, t)
    nb = -(-t // blk)
    hb = jnp.pad(h.reshape(t, d), ((0, nb * blk - t), (0, 0))).reshape(nb, blk, d)

    def one(xb):
        q = (xb @ w_q).reshape(blk, PEER_HEADS, 2, PEER_HALF).astype(F32)
        sc = jnp.einsum('thpk,pmk->thpm', q, sub_keys.astype(F32))
        vals, idx = lax.top_k(sc, PEER_TOPK)
        cand = (vals[:, :, 0, :, None] + vals[:, :, 1, None, :]).reshape(blk, PEER_HEADS, -1)
        cid = (idx[:, :, 0, :, None] * PEER_NKEYS + idx[:, :, 1, None, :]).reshape(blk, PEER_HEADS, -1)
        top, sel = lax.top_k(cand, PEER_TOPK)
        eid = jnp.take_along_axis(cid, sel, axis=-1)
        g = jax.nn.softmax(top, axis=-1)
        u, v = u_tab[eid], v_tab[eid]
        a = jax.nn.gelu(jnp.einsum('td,thkd->thk', xb, u, preferred_element_type=F32), approximate=False)
        return jnp.einsum('thk,thkd->td', (g * a).astype(xb.dtype), v)

    out = lax.map(one, hb).reshape(nb * blk, d)[:t]
    return out.reshape(n, l, d)


def setup_inputs(seed: int = 0) -> dict:
    key = jax.random.key(seed)
    ks = iter(jax.random.split(key, 40))

    def nrm(shape, scale):
        return jax.random.normal(next(ks), shape, F32) * scale

    D = D_MODEL
    buf = [min(w, PAST_LEN) for w, _ in DIL_PATTERNS]
    kvs = (DIL_HEADS, DIL_HEAD_DIM)
    return {
        'x_prompt': nrm((BATCH, SEQ, D), 1.0),
        'x_sample': nrm((DEC_BATCH, DEC_SEQ, D), 1.0),
        'state_pool': nrm((N_POOL_LAYERS, DEC_BATCH, POOL_HIST, D), 1.0),
        'cache_dil_kv0': nrm((N_DIL_LAYERS, DEC_BATCH, buf[0], 2) + kvs, 1.0),
        'cache_dil_kv1': nrm((N_DIL_LAYERS, DEC_BATCH, buf[1], 2) + kvs, 1.0),
        'cache_dil_kv2': nrm((N_DIL_LAYERS, DEC_BATCH, buf[2], 2) + kvs, 1.0),
        'state_ret': nrm((N_RET_LAYERS, DEC_BATCH, RET_HEADS, RET_DK, RET_DV), 1.0),
        'c_prompt': nrm((BATCH, D), 1.0),
        'c_sample': nrm((DEC_BATCH, D), 1.0),
        'norm1_g': 1.0 + nrm((DEPTH, D), 0.1),
        'norm2_g': 1.0 + nrm((DEPTH, D), 0.1),
        'mod_w': nrm((DEPTH, D, 6 * D), 0.5 * D ** -0.5),
        'mod_b': nrm((DEPTH, 6 * D), 0.02),
        'pool_w': nrm((N_POOL_LAYERS, len(POOL_WINDOWS), POOL_GROUP, POOL_GROUP), POOL_GROUP ** -0.5),
        'pool_scale': 1.0 + nrm((N_POOL_LAYERS, D), 0.1),
        'dil_w_in': nrm((N_DIL_LAYERS, D, 3 * DIL_WIDTH), D ** -0.5),
        'dil_w_out': nrm((N_DIL_LAYERS, DIL_HEADS * DIL_HEAD_DIM, D), (DIL_HEADS * DIL_HEAD_DIM) ** -0.5),
        'ret_w_in': nrm((N_RET_LAYERS, D, 2 * RET_QK + 2 * RET_V), D ** -0.5),
        'ret_gn_g': 1.0 + nrm((N_RET_LAYERS, RET_V), 0.1),
        'ret_w_out': nrm((N_RET_LAYERS, RET_V, D), RET_V ** -0.5),
        'peer_w_q': nrm((DEPTH, D, PEER_HEADS * PEER_KEY_DIM), D ** -0.5),
        'peer_keys': nrm((DEPTH, 2, PEER_NKEYS, PEER_HALF), PEER_HALF ** -0.5),
        'peer_u': nrm((DEPTH, PEER_EXPERTS, D), D ** -0.5),
        'peer_v': nrm((DEPTH, PEER_EXPERTS, D), PEER_HEADS ** -0.5),
        'final_g': 1.0 + nrm((D,), 0.1),
    }


def reference(x_prompt, x_sample, state_pool, cache_dil_kv0, cache_dil_kv1, cache_dil_kv2, state_ret,
              c_prompt, c_sample, norm1_g, norm2_g, mod_w, mod_b, pool_w, pool_scale,
              dil_w_in, dil_w_out, ret_w_in, ret_gn_g, ret_w_out,
              peer_w_q, peer_keys, peer_u, peer_v, final_g):
    dil_caches = (cache_dil_kv0, cache_dil_kv1, cache_dil_kv2)
    xp, xs = x_prompt, x_sample
    pool_p, pool_s = [], []
    dil_p = [[] for _ in range(DIL_GROUPS)]
    dil_s = [[] for _ in range(DIL_GROUPS)]
    ret_p, ret_s = [], []
    for i in range(DEPTH):
        kind, j = i % N_MIXERS, i // N_MIXERS
        mp = adaln(c_prompt, mod_w[i], mod_b[i])
        ms = adaln(c_sample, mod_w[i], mod_b[i])
        hp = modulate(rmsnorm(xp, norm1_g[i]), mp[0], mp[1])
        hs = modulate(rmsnorm(xs, norm1_g[i]), ms[0], ms[1])
        if kind == 0:
            yp, st_p = pool_mixer(hp, None, 0, pool_w[j], pool_scale[j])
            ys, st_s = pool_mixer(hs, state_pool[j], PAST_LEN, pool_w[j], pool_scale[j])
            pool_p.append(st_p)
            pool_s.append(st_s)
        elif kind == 1:
            yp, kv_p = dilated_mixer(hp, None, dil_w_in[j], dil_w_out[j])
            ys, kv_s = dilated_mixer(hs, [cc[j] for cc in dil_caches], dil_w_in[j], dil_w_out[j])
            for g in range(DIL_GROUPS):
                dil_p[g].append(kv_p[g])
                dil_s[g].append(kv_s[g])
        else:
            yp, st_p = retention_mixer(hp, None, ret_w_in[j], ret_gn_g[j], ret_w_out[j])
            ys, st_s = retention_mixer(hs, state_ret[j], ret_w_in[j], ret_gn_g[j], ret_w_out[j])
            ret_p.append(st_p)
            ret_s.append(st_s)
        xp = xp + mp[2] * yp
        xs = xs + ms[2] * ys
        hp = modulate(rmsnorm(xp, norm2_g[i]), mp[3], mp[4])
        hs = modulate(rmsnorm(xs, norm2_g[i]), ms[3], ms[4])
        xp = xp + mp[5] * peer(hp, peer_w_q[i], peer_keys[i], peer_u[i], peer_v[i])
        xs = xs + ms[5] * peer(hs, peer_w_q[i], peer_keys[i], peer_u[i], peer_v[i])
    y_prompt = rmsnorm(xp, final_g)
    y_sample = rmsnorm(xs, final_g)
    return (y_prompt, y_sample,
            jnp.stack(pool_p), jnp.stack(pool_s),
            jnp.stack(dil_p[0]), jnp.stack(dil_s[0]),
            jnp.stack(dil_p[1]), jnp.stack(dil_s[1]),
            jnp.stack(dil_p[2]), jnp.stack(dil_s[2]),
            jnp.stack(ret_p), jnp.stack(ret_s))
```

```python
import functools
import math

import jax
import jax.numpy as jnp
import numpy as np
from jax import lax
from jax.experimental import pallas as pl
from jax.experimental.pallas import tpu as pltpu

F32 = jnp.float32
BF16 = jnp.bfloat16
EPS = 1e-6
NEG_INF = float("-inf")

LANES = 128
SUBLANES = 8
VMEM_LIMIT_BYTES = 56 * 1024 * 1024

POOL_WINDOWS = (2, 4, 8, 16)
POOL_HIST = 15
DIL_PATTERNS = ((128, 1), (512, 4), (2048, 16))
DIL_HEADS = 8
DIL_HEAD_DIM = 128
DIL_BLOCK = 128
RET_HEADS = 8
RET_CHUNK = 128
PEER_HEADS = 8
PEER_NKEYS = 128
PEER_TOPK = 16
SLOT = 512


def _cparams(sem, vmem=VMEM_LIMIT_BYTES):
    return pltpu.CompilerParams(dimension_semantics=sem, vmem_limit_bytes=vmem)


def _rms_mod(x, g, shift, scale):
    ms = jnp.mean(x * x, axis=-1, keepdims=True)
    y = x * lax.rsqrt(ms + EPS) * g
    return y * (1.0 + scale) + shift


def _dot(a, b):
    return jnp.dot(a, b, preferred_element_type=F32)


def _dot_nt(a, b):
    return lax.dot_general(a, b, (((1,), (1,)), ((), ())), preferred_element_type=F32)


def _dot_tn(a, b):
    return lax.dot_general(a, b, (((0,), (0,)), ((), ())), preferred_element_type=F32)


def _adaln_kernel(c_ref, w_ref, b_ref, o_ref):
    c = c_ref[...]
    a = c * (1.0 / (1.0 + jnp.exp(-c)))
    a_hi = a.astype(BF16)
    a_lo = (a - a_hi.astype(F32)).astype(BF16)
    w = w_ref[...]
    w_hi = w.astype(BF16)
    w_lo = (w - w_hi.astype(F32)).astype(BF16)
    acc = _dot(a_hi, w_hi) + _dot(a_hi, w_lo) + _dot(a_lo, w_hi)
    o_ref[...] = acc + b_ref[...]


def _adaln(c_all, mod_w, mod_b, tn=1024):
    depth, d, n = mod_w.shape
    rows = c_all.shape[0]
    return pl.pallas_call(
        _adaln_kernel,
        grid=(depth, n // tn),
        in_specs=[
            pl.BlockSpec((rows, d), lambda l, j: (0, 0)),
            pl.BlockSpec((None, d, tn), lambda l, j: (l, 0, j)),
            pl.BlockSpec((None, 1, tn), lambda l, j: (l, 0, j)),
        ],
        out_specs=pl.BlockSpec((None, rows, tn), lambda l, j: (l, 0, j)),
        out_shape=jax.ShapeDtypeStruct((depth, rows, n), F32),
        compiler_params=_cparams(("arbitrary", "arbitrary")),
        name="adaln",
    )(c_all, mod_w, mod_b.reshape(depth, 1, n))


def _proj_in_kernel(x_ref, g_ref, sh_ref, sc_ref, w_ref, o_ref, h_ref):
    @pl.when(pl.program_id(1) == 0)
    def _():
        h_ref[...] = _rms_mod(x_ref[...], g_ref[...], sh_ref[...], sc_ref[...]).astype(BF16)

    o_ref[...] = _dot(h_ref[...], w_ref[...])


def _proj_in(x, g, shift, scale, w_bf, tm, rows_per_mod):
    t, d = x.shape
    n = w_bf.shape[1]
    r = shift.shape[1]
    tiles_per_mod = rows_per_mod // tm
    mod_spec = pl.BlockSpec((None, r, d), lambda i, j: (i // tiles_per_mod, 0, 0))
    return pl.pallas_call(
        _proj_in_kernel,
        grid=(t // tm, n // SLOT),
        in_specs=[
            pl.BlockSpec((tm, d), lambda i, j: (i, 0)),
            pl.BlockSpec((1, d), lambda i, j: (0, 0)),
            mod_spec,
            mod_spec,
            pl.BlockSpec((d, SLOT), lambda i, j: (0, j)),
        ],
        out_specs=pl.BlockSpec((None, tm, SLOT), lambda i, j: (j, i, 0)),
        out_shape=jax.ShapeDtypeStruct((n // SLOT, t, SLOT), F32),
        scratch_shapes=[pltpu.VMEM((tm, d), BF16)],
        compiler_params=_cparams(("arbitrary", "arbitrary")),
        name="proj_in",
    )(x, g, shift, scale, w_bf)


def _proj_out_kernel(a_ref, w_ref, x_ref, gate_ref, o_ref):
    y = _dot(a_ref[...].astype(BF16), w_ref[...])
    o_ref[...] = x_ref[...] + gate_ref[...] * y


def _proj_out(a, w_bf, x, gate, tm, rows_per_mod, tn=SLOT):
    t, k = a.shape
    d = w_bf.shape[1]
    r = gate.shape[1]
    tiles_per_mod = rows_per_mod // tm
    return pl.pallas_call(
        _proj_out_kernel,
        grid=(t // tm, d // tn),
        in_specs=[
            pl.BlockSpec((tm, k), lambda i, j: (i, 0)),
            pl.BlockSpec((k, tn), lambda i, j: (0, j)),
            pl.BlockSpec((tm, tn), lambda i, j: (i, j)),
            pl.BlockSpec((None, r, tn), lambda i, j: (i // tiles_per_mod, 0, j)),
        ],
        out_specs=pl.BlockSpec((tm, tn), lambda i, j: (i, j)),
        out_shape=jax.ShapeDtypeStruct((t, d), F32),
        compiler_params=_cparams(("arbitrary", "arbitrary")),
        name="proj_out",
    )(a, w_bf, x, gate)


CARRY = 16


def _pool_kernel(x_ref, g_ref, sh_ref, sc_ref, gate_ref, hist_ref, w_ref, ps_ref,
                 o_ref, st_ref, ext_ref, *, tr, n_real, pos0):
    t = pl.program_id(1)
    d = x_ref.shape[-1]
    grp = d // len(POOL_WINDOWS)

    @pl.when(t == 0)
    def _():
        ext_ref[0:CARRY, :] = hist_ref[...]

    x = x_ref[...]
    h = _rms_mod(x, g_ref[...], sh_ref[...], sc_ref[...])
    ext_ref[CARRY:CARRY + tr, :] = h

    pos = (pos0 + t * tr + lax.broadcasted_iota(jnp.int32, (tr, 1), 0)).astype(F32)
    ys = []
    for gi, wdw in enumerate(POOL_WINDOWS):
        c0 = gi * grp
        cur = ext_ref[CARRY:CARRY + tr, c0:c0 + grp]
        acc = cur
        for k in range(1, wdw):
            acc = acc + ext_ref[CARRY - k:CARRY - k + tr, c0:c0 + grp]
        cnt = jnp.minimum(float(wdw), pos + 1.0)
        dd = acc / cnt - cur
        ys.append(_dot(dd.astype(BF16), w_ref[gi]))
    y = jnp.concatenate(ys, axis=-1) * ps_ref[...]
    o_ref[...] = x + gate_ref[...] * y

    st_ref[...] = ext_ref[n_real:n_real + CARRY, :]
    ext_ref[0:CARRY, :] = ext_ref[tr:tr + CARRY, :]


def _pool_layer(x, g, shift, scale, gate, hist16, w_bf, pscale, *, tr, n_real, pos0):
    b, s, d = x.shape
    grp = d // len(POOL_WINDOWS)
    mod_spec = pl.BlockSpec((None, 1, d), lambda i, t: (i, 0, 0))
    kern = functools.partial(_pool_kernel, tr=tr, n_real=n_real, pos0=pos0)
    return pl.pallas_call(
        kern,
        grid=(b, s // tr),
        in_specs=[
            pl.BlockSpec((None, tr, d), lambda i, t: (i, t, 0)),
            pl.BlockSpec((1, d), lambda i, t: (0, 0)),
            mod_spec, mod_spec, mod_spec,
            pl.BlockSpec((None, CARRY, d), lambda i, t: (i, 0, 0)),
            pl.BlockSpec((len(POOL_WINDOWS), grp, grp), lambda i, t: (0, 0, 0)),
            pl.BlockSpec((1, d), lambda i, t: (0, 0)),
        ],
        out_specs=[
            pl.BlockSpec((None, tr, d), lambda i, t: (i, t, 0)),
            pl.BlockSpec((None, CARRY, d), lambda i, t: (i, 0, 0)),
        ],
        out_shape=[
            jax.ShapeDtypeStruct((b, s, d), F32),
            jax.ShapeDtypeStruct((b, CARRY, d), F32),
        ],
        scratch_shapes=[pltpu.VMEM((CARRY + tr, d), F32)],
        compiler_params=_cparams(("arbitrary", "arbitrary")),
        name="pool",
    )(x, g, shift, scale, gate, hist16, w_bf, pscale)


def _alibi_table():
    n = len(DIL_PATTERNS) * DIL_HEADS
    slopes = np.exp2(-8.0 * np.arange(1, n + 1, dtype=np.float64) / n).reshape(len(DIL_PATTERNS), DIL_HEADS)
    dil = np.array([p[1] for p in DIL_PATTERNS], np.float64)[:, None]
    tab = (slopes * dil).T
    return np.broadcast_to(tab[:, :, None], (DIL_HEADS, len(DIL_PATTERNS), LANES)).astype(np.float32)


def _dil_prompt_kernel(*refs, seq):
    q_refs = refs[0:3]
    k_refs = refs[3:6]
    v_refs = refs[6:9]
    slope_ref = refs[9]
    o_ref = refs[10]
    qd, kd, vd, og, lg = refs[11:16]
    oi = refs[16:19]
    li = refs[19:22]
    blk = DIL_BLOCK
    hd = DIL_HEAD_DIM
    scale = hd ** -0.5

    row = lax.broadcasted_iota(jnp.int32, (blk, blk), 0)
    col = lax.broadcasted_iota(jnp.int32, (blk, blk), 1)
    dist_c = (row - col).astype(F32)
    mask_c = row >= col
    dist_p = dist_c + float(blk)
    mask_p = col >= row

    for g, (wdw, dil) in enumerate(DIL_PATTERNS):
        ls = seq // dil
        nb = ls // blk
        if dil == 1:
            qs, ks, vs = q_refs[g], k_refs[g], v_refs[g]
        else:
            for res in range(dil):
                qd[res * ls:(res + 1) * ls, :] = q_refs[g][pl.ds(res, ls, stride=dil), :]
                kd[res * ls:(res + 1) * ls, :] = k_refs[g][pl.ds(res, ls, stride=dil), :]
                vd[res * ls:(res + 1) * ls, :] = v_refs[g][pl.ds(res, ls, stride=dil), :]
            qs, ks, vs = qd, kd, vd
        slope = slope_ref[g:g + 1, :]
        bias_c = slope * dist_c
        bias_p = slope * dist_p

        def step(it, carry, qs=qs, ks=ks, vs=vs, nb=nb, bias_c=bias_c, bias_p=bias_p):
            r0 = pl.multiple_of(it * blk, blk)
            has_prev = (it % nb) != 0
            rp = pl.multiple_of(jnp.maximum(it - 1, 0) * blk, blk)
            qb = qs[pl.ds(r0, blk), :].astype(BF16)
            kc = ks[pl.ds(r0, blk), :].astype(BF16)
            kp = ks[pl.ds(rp, blk), :].astype(BF16)
            s_c = _dot_nt(qb, kc) * scale - bias_c
            s_p = _dot_nt(qb, kp) * scale - bias_p
            pen = jnp.where(has_prev, 0.0, NEG_INF)
            s_c = jnp.where(mask_c, s_c, NEG_INF)
            s_p = jnp.where(mask_p, s_p + pen, NEG_INF)
            m = jnp.maximum(jnp.max(s_c, axis=-1, keepdims=True), jnp.max(s_p, axis=-1, keepdims=True))
            p_c = jnp.exp(s_c - m)
            p_p = jnp.exp(s_p - m)
            den = jnp.sum(p_c, axis=-1, keepdims=True) + jnp.sum(p_p, axis=-1, keepdims=True)
            vc = vs[pl.ds(r0, blk), :].astype(BF16)
            vp = vs[pl.ds(rp, blk), :].astype(BF16)
            o = (_dot(p_c.astype(BF16), vc) + _dot(p_p.astype(BF16), vp)) / den
            og[pl.ds(r0, blk), :] = o
            lg[pl.ds(r0, blk), :] = jnp.broadcast_to(m + jnp.log(den), (blk, hd))
            return carry

        lax.fori_loop(0, seq // blk, step, 0)

        if dil == 1:
            oi[g][...] = og[...]
            li[g][...] = lg[...]
        else:
            for res in range(dil):
                oi[g][pl.ds(res, ls, stride=dil), :] = og[res * ls:(res + 1) * ls, :]
                li[g][pl.ds(res, ls, stride=dil), :] = lg[res * ls:(res + 1) * ls, :]

    l0, l1, l2 = li[0][...], li[1][...], li[2][...]
    mx = jnp.maximum(jnp.maximum(l0, l1), l2)
    e0, e1, e2 = jnp.exp(l0 - mx), jnp.exp(l1 - mx), jnp.exp(l2 - mx)
    o_ref[...] = (e0 * oi[0][...] + e1 * oi[1][...] + e2 * oi[2][...]) / (e0 + e1 + e2)


def _dil_prompt_attn(slots, slope_tab, batch, seq):
    hd = DIL_HEAD_DIM
    per_slot = SLOT // hd
    ng = len(DIL_PATTERNS)
    slots_per_group = (DIL_HEADS * hd) // SLOT

    def spec(kind, g):
        base = (kind * ng + g) * slots_per_group
        return pl.BlockSpec((None, seq, hd), lambda n, h: (base + h // per_slot, n, h % per_slot))

    in_specs = [spec(kind, g) for kind in range(3) for g in range(ng)]
    in_specs.append(pl.BlockSpec((None, ng, LANES), lambda n, h: (h, 0, 0)))
    kern = functools.partial(_dil_prompt_kernel, seq=seq)
    return pl.pallas_call(
        kern,
        grid=(batch, DIL_HEADS),
        in_specs=in_specs,
        out_specs=pl.BlockSpec((seq, hd), lambda n, h: (n, h)),
        out_shape=jax.ShapeDtypeStruct((batch * seq, DIL_HEADS * hd), F32),
        scratch_shapes=[pltpu.VMEM((seq, hd), F32) for _ in range(5 + 2 * ng)],
        compiler_params=_cparams(("arbitrary", "arbitrary")),
        name="dil_attn_prompt",
    )(*([slots] * 9), slope_tab)


def _dil_sample_kernel(slots_ref, c0_ref, c1_ref, c2_ref, slope_ref, o_ref):
    n = pl.program_id(0)
    hd = DIL_HEAD_DIM
    per_slot = SLOT // hd
    ng = len(DIL_PATTERNS)
    spg = (DIL_HEADS * hd) // SLOT
    scale = hd ** -0.5
    caches = (c0_ref, c1_ref, c2_ref)
    nk = DIL_BLOCK
    steps = (nk - lax.broadcasted_iota(jnp.int32, (nk, 1), 0)).astype(F32)

    outs = []
    for h in range(DIL_HEADS):
        sub = (h % per_slot) * hd

        def row(kind, g):
            full = slots_ref[(kind * ng + g) * spg + h // per_slot, pl.ds(n, 1), :]
            return full[:, sub:sub + hd]

        o_g, l_g = [], []
        for g in range(ng):
            qh, kn, vn = row(0, g), row(1, g), row(2, g)
            kc = caches[g][:, h * hd:(h + 1) * hd]
            vc = caches[g][:, DIL_HEADS * hd + h * hd:DIL_HEADS * hd + (h + 1) * hd]
            slope = slope_ref[h, g:g + 1, 0:1]
            s_c = jnp.sum(kc * qh, axis=-1, keepdims=True) * scale - slope * steps
            s_n = jnp.sum(kn * qh, axis=-1, keepdims=True) * scale
            m = jnp.maximum(jnp.max(s_c, axis=0, keepdims=True), s_n)
            p_c = jnp.exp(s_c - m)
            p_n = jnp.exp(s_n - m)
            den = jnp.sum(p_c, axis=0, keepdims=True) + p_n
            o = (jnp.sum(p_c * vc, axis=0, keepdims=True) + p_n * vn) / den
            o_g.append(o)
            l_g.append(m + jnp.log(den))
        mx = jnp.maximum(jnp.maximum(l_g[0], l_g[1]), l_g[2])
        e = [jnp.exp(l - mx) for l in l_g]
        outs.append((e[0] * o_g[0] + e[1] * o_g[1] + e[2] * o_g[2]) / (e[0] + e[1] + e[2]))
    o_ref[...] = jnp.concatenate(outs, axis=-1)


def _dil_sample_attn(slots, caches, slope_tab):
    b = caches[0].shape[0]
    width = 2 * DIL_HEADS * DIL_HEAD_DIM
    ins = []
    specs = [pl.BlockSpec(slots.shape, lambda n: (0, 0, 0))]
    for g, (wdw, dil) in enumerate(DIL_PATTERNS):
        ins.append(caches[g].reshape(b, wdw // dil, dil * width))
        specs.append(pl.BlockSpec((None, wdw // dil, width), lambda n: (n, 0, 0)))
    specs.append(pl.BlockSpec(slope_tab.shape, lambda n: (0, 0, 0)))
    return pl.pallas_call(
        _dil_sample_kernel,
        grid=(b,),
        in_specs=specs,
        out_specs=pl.BlockSpec((None, 1, DIL_HEADS * DIL_HEAD_DIM), lambda n: (n, 0, 0)),
        out_shape=jax.ShapeDtypeStruct((b, 1, DIL_HEADS * DIL_HEAD_DIM), F32),
        compiler_params=_cparams(("arbitrary",)),
        name="dil_attn_sample",
    )(slots, *ins, slope_tab)


def _ret_tables(dk, dv):
    c = RET_CHUNK
    log_g = np.log1p(-np.exp2(-5.0 - np.arange(RET_HEADS, dtype=np.float64)))
    pos = np.arange(c, dtype=np.float64)
    diff = pos[:, None] - pos[None, :]
    dmask = np.where(diff >= 0, np.exp(log_g[:, None, None] * np.maximum(diff, 0.0)), 0.0)
    q_dec = np.exp(log_g[:, None] * (pos[None, :] + 1.0))
    k_dec = np.exp(log_g[:, None] * (c - 1.0 - pos[None, :]))
    c_dec = np.exp(log_g * c)
    gam = np.exp(log_g)
    f = lambda a: jnp.asarray(a.astype(np.float32))
    return dict(
        dmask=f(dmask),
        q_dec=f(np.broadcast_to(q_dec[:, :, None], (RET_HEADS, c, dk))),
        k_dec=f(np.broadcast_to(k_dec[:, :, None], (RET_HEADS, c, dk))),
        c_dec=f(np.broadcast_to(c_dec[:, None, None], (RET_HEADS, 1, LANES))),
        gamma=f(np.broadcast_to(gam[:, None, None], (RET_HEADS, 1, LANES))),
    )


def _gn_gate(o, gn, gate):
    mu = jnp.mean(o, axis=-1, keepdims=True)
    cen = o - mu
    var = jnp.mean(cen * cen, axis=-1, keepdims=True)
    y = cen * lax.rsqrt(var + EPS) * gn
    return gate * (1.0 / (1.0 + jnp.exp(-gate))) * y


def _ret_prompt_kernel(q_ref, k_ref, v_ref, gt_ref, dm_ref, qd_ref, kd_ref, cd_ref, gn_ref,
                       y_ref, st_ref, state_ref, *, seq):
    c = RET_CHUNK
    dk = q_ref.shape[-1]
    state_ref[...] = jnp.zeros_like(state_ref)
    dmask = dm_ref[...]
    q_dec = qd_ref[...]
    k_dec = kd_ref[...]
    c_dec = cd_ref[:, 0:1]
    gn = gn_ref[...]
    kscale = dk ** -0.5

    def step(ci, carry):
        r0 = pl.multiple_of(ci * c, c)
        qc = q_ref[pl.ds(r0, c), :]
        kc = k_ref[pl.ds(r0, c), :] * kscale
        vc = v_ref[pl.ds(r0, c), :].astype(BF16)
        state = state_ref[...]
        att = _dot_nt(qc.astype(BF16), kc.astype(BF16)) * dmask
        o = _dot(att.astype(BF16), vc) + _dot((qc * q_dec).astype(BF16), state.astype(BF16))
        state_ref[...] = c_dec * state + _dot_tn((kc * k_dec).astype(BF16), vc)
        y_ref[pl.ds(r0, c), :] = _gn_gate(o, gn, gt_ref[pl.ds(r0, c), :]).astype(y_ref.dtype)
        return carry

    lax.fori_loop(0, seq // c, step, 0)
    st_ref[...] = state_ref[...]


def _ret_prompt(slots, tabs, gn, batch, seq, dk, dv):
    hq = SLOT // dk
    nq = RET_HEADS // hq
    vps = dv // SLOT
    assert vps == 1
    kern = functools.partial(_ret_prompt_kernel, seq=seq)
    return pl.pallas_call(
        kern,
        grid=(batch, RET_HEADS),
        in_specs=[
            pl.BlockSpec((None, seq, dk), lambda n, h: (h // hq, n, h % hq)),
            pl.BlockSpec((None, seq, dk), lambda n, h: (nq + h // hq, n, h % hq)),
            pl.BlockSpec((None, seq, dv), lambda n, h: (2 * nq + h, n, 0)),
            pl.BlockSpec((None, seq, dv), lambda n, h: (2 * nq + RET_HEADS + h, n, 0)),
            pl.BlockSpec((None, RET_CHUNK, RET_CHUNK), lambda n, h: (h, 0, 0)),
            pl.BlockSpec((None, RET_CHUNK, dk), lambda n, h: (h, 0, 0)),
            pl.BlockSpec((None, RET_CHUNK, dk), lambda n, h: (h, 0, 0)),
            pl.BlockSpec((None, 1, LANES), lambda n, h: (h, 0, 0)),
            pl.BlockSpec((1, dv), lambda n, h: (0, h)),
        ],
        out_specs=[
            pl.BlockSpec((seq, dv), lambda n, h: (n, h)),
            pl.BlockSpec((None, None, dk, dv), lambda n, h: (n, h, 0, 0)),
        ],
        out_shape=[
            jax.ShapeDtypeStruct((batch * seq, RET_HEADS * dv), BF16),
            jax.ShapeDtypeStruct((batch, RET_HEADS, dk, dv), F32),
        ],
        scratch_shapes=[pltpu.VMEM((dk, dv), F32)],
        compiler_params=_cparams(("arbitrary", "arbitrary")),
        name="ret_prompt",
    )(slots, slots, slots, slots, tabs["dmask"], tabs["q_dec"], tabs["k_dec"], tabs["c_dec"], gn)


def _ret_sample_kernel(slots_ref, s0_ref, gam_ref, gn_ref, y_ref, st_ref, *, dk, dv):
    n = pl.program_id(0)
    h = pl.program_id(1)
    hq = SLOT // dk
    nq = RET_HEADS // hq
    gam = gam_ref[:, 0:1]

    def qk_row(base):
        full = slots_ref[base + h // hq, pl.ds(n, 1), :]
        halves = [full[:, i * dk:(i + 1) * dk] for i in range(hq)]
        out = halves[0]
        for i in range(1, hq):
            out = jnp.where(h % hq == i, halves[i], out)
        return out

    q = qk_row(0)
    k = qk_row(nq) * (dk ** -0.5)
    v = slots_ref[2 * nq + h, pl.ds(n, 1), :]
    gate = slots_ref[2 * nq + RET_HEADS + h, pl.ds(n, 1), :]

    eye = lax.broadcasted_iota(jnp.int32, (dk, dk), 0) == lax.broadcasted_iota(jnp.int32, (dk, dk), 1)
    q_col = jnp.sum(jnp.where(eye, q, 0.0), axis=-1, keepdims=True)
    k_col = jnp.sum(jnp.where(eye, k, 0.0), axis=-1, keepdims=True)
    state = s0_ref[...]
    qk = jnp.sum(q * k, axis=-1, keepdims=True)
    o = qk * v + gam * jnp.sum(q_col * state, axis=0, keepdims=True)
    st_ref[...] = gam * state + k_col * v
    y_ref[...] = _gn_gate(o, gn_ref[...], gate)


def _ret_sample(slots, s0, tabs, gn, dk, dv):
    b = s0.shape[0]
    kern = functools.partial(_ret_sample_kernel, dk=dk, dv=dv)
    return pl.pallas_call(
        kern,
        grid=(b, RET_HEADS),
        in_specs=[
            pl.BlockSpec(slots.shape, lambda n, h: (0, 0, 0)),
            pl.BlockSpec((None, None, dk, dv), lambda n, h: (n, h, 0, 0)),
            pl.BlockSpec((None, 1, LANES), lambda n, h: (h, 0, 0)),
            pl.BlockSpec((1, dv), lambda n, h: (0, h)),
        ],
        out_specs=[
            pl.BlockSpec((None, 1, dv), lambda n, h: (n, 0, h)),
            pl.BlockSpec((None, None, dk, dv), lambda n, h: (n, h, 0, 0)),
        ],
        out_shape=[
            jax.ShapeDtypeStruct((b, 1, RET_HEADS * dv), F32),
            jax.ShapeDtypeStruct((b, RET_HEADS, dk, dv), F32),
        ],
        compiler_params=_cparams(("arbitrary", "arbitrary")),
        name="ret_sample",
    )(slots, s0, tabs["gamma"], gn)


def _cand_tables():
    k = PEER_TOPK
    rows, cols = [], []
    rows += [0] * k
    cols += list(range(k))
    for r in range(1, SUBLANES):
        rows += [r] * SUBLANES
        cols += list(range(SUBLANES))
    rows += list(range(SUBLANES, k))
    cols += [0] * (k - SUBLANES)
    rows, cols = np.array(rows), np.array(cols)
    valid = (rows + 1) * (cols + 1) <= k
    pos = (rows * k + cols).astype(np.float32)
    bias = np.where(valid, 0.0, -np.inf).astype(np.float32)
    f = lambda a: jnp.asarray(np.broadcast_to(a[:, None], (a.shape[0], LANES)).copy())
    return f(pos), f(bias)


def _stack_rows(rows):
    n = len(rows)
    idx = lax.broadcasted_iota(jnp.int32, (n, rows[0].shape[-1]), 0)
    out = jnp.broadcast_to(rows[n - 1], idx.shape)
    for i in range(n - 2, -1, -1):
        out = jnp.where(idx == i, rows[i], out)
    return out


def _extract_topk(s, pos, k):
    rank = jnp.full(s.shape, float(k), F32)
    vals = []
    big = float(2 ** 20)
    for r in range(k):
        m = jnp.max(s, axis=0, keepdims=True)
        first = jnp.min(jnp.where(s == m, pos, big), axis=0, keepdims=True)
        hit = pos == first
        rank = jnp.where(hit, float(r), rank)
        s = jnp.where(hit, NEG_INF, s)
        vals.append(m)
    return rank, vals


def _peer_select_kernel(x_ref, g_ref, sh_ref, sc_ref, wq_ref, keys_ref, cpos_ref, cbias_ref,
                        h_ref, rank1_ref, nsel_ref, ea_ref, eb_ref, s_ref, *, tp):
    nk = PEER_NKEYS
    k = PEER_TOPK
    h = _rms_mod(x_ref[...], g_ref[...], sh_ref[...], sc_ref[...])
    hb = h.astype(BF16)
    h_ref[...] = hb
    q_t = _dot_nt(wq_ref[...], hb)
    for hp in range(2 * PEER_HEADS):
        key = keys_ref[hp % 2].astype(BF16)
        s_ref[hp] = _dot(key, q_t[hp * nk:(hp + 1) * nk, :].astype(BF16))

    key_pos = lax.broadcasted_iota(jnp.int32, (nk, LANES), 0).astype(F32)
    cpos = cpos_ref[...]
    cbias = cbias_ref[...]

    def body(it, carry):
        head = it // (tp // LANES)
        lg = it % (tp // LANES)
        l0 = pl.multiple_of(lg * LANES, LANES)
        s0 = s_ref[2 * head, :, pl.ds(l0, LANES)]
        s1 = s_ref[2 * head + 1, :, pl.ds(l0, LANES)]
        rank0, a = _extract_topk(s0, key_pos, k)
        rank1, b = _extract_topk(s1, key_pos, k)
        a_hi = _stack_rows(a[SUBLANES:])
        b_all = _stack_rows(b)
        b_lo = b_all[0:SUBLANES]
        cand = [a[0] + b_all] + [a[r] + b_lo for r in range(1, SUBLANES)] + [a_hi + b[0]]
        cand = jnp.concatenate(cand, axis=0) + cbias
        crank, _ = _extract_topk(cand, cpos, k)
        sel = (crank < float(k)).astype(F32)
        top = a[0] + b[0]
        z = jnp.sum(sel * jnp.exp(cand - top), axis=0, keepdims=True)
        n_r = [jnp.sum(sel[0:k], axis=0, keepdims=True)]
        for r in range(1, SUBLANES):
            lo = k + (r - 1) * SUBLANES
            n_r.append(jnp.sum(sel[lo:lo + SUBLANES], axis=0, keepdims=True))
        lo = k + (SUBLANES - 1) * SUBLANES
        for r in range(SUBLANES, k):
            n_r.append(sel[lo + r - SUBLANES:lo + r - SUBLANES + 1])
        nsel = jnp.zeros((nk, LANES), F32)
        for r in range(k):
            nsel = jnp.where(rank0 == float(r), n_r[r], nsel)
        rank1_ref[head, :, pl.ds(l0, LANES)] = rank1
        nsel_ref[head, :, pl.ds(l0, LANES)] = nsel
        ea_ref[head, :, pl.ds(l0, LANES)] = jnp.exp(s0 - a[0]) / z
        eb_ref[head, :, pl.ds(l0, LANES)] = jnp.exp(s1 - b[0])
        return carry

    lax.fori_loop(0, PEER_HEADS * (tp // LANES), body, 0)


def _peer_select(x, g, shift, scale, wq_t_bf, keys, cpos, cbias, tp, rows_per_mod):
    t, d = x.shape
    r = shift.shape[1]
    tiles_per_mod = rows_per_mod // tp
    nk = PEER_NKEYS
    mod_spec = pl.BlockSpec((None, r, d), lambda i: (i // tiles_per_mod, 0, 0))
    tab_spec = pl.BlockSpec((PEER_HEADS, nk, tp), lambda i: (0, 0, i))
    tab_shape = jax.ShapeDtypeStruct((PEER_HEADS, nk, t), F32)
    kern = functools.partial(_peer_select_kernel, tp=tp)
    return pl.pallas_call(
        kern,
        grid=(t // tp,),
        in_specs=[
            pl.BlockSpec((tp, d), lambda i: (i, 0)),
            pl.BlockSpec((1, d), lambda i: (0, 0)),
            mod_spec, mod_spec,
            pl.BlockSpec(wq_t_bf.shape, lambda i: (0, 0)),
            pl.BlockSpec(keys.shape, lambda i: (0, 0, 0)),
            pl.BlockSpec(cpos.shape, lambda i: (0, 0)),
            pl.BlockSpec(cbias.shape, lambda i: (0, 0)),
        ],
        out_specs=[pl.BlockSpec((tp, d), lambda i: (i, 0)), tab_spec, tab_spec, tab_spec, tab_spec],
        out_shape=[jax.ShapeDtypeStruct((t, d), BF16), tab_shape, tab_shape, tab_shape, tab_shape],
        scratch_shapes=[pltpu.VMEM((2 * PEER_HEADS, nk, tp), F32)],
        compiler_params=_cparams(("arbitrary",)),
        name="peer_select",
    )(x, g, shift, scale, wq_t_bf, keys, cpos, cbias)


def _peer_dense_kernel(h_ref, rank1_ref, nsel_ref, ea_ref, eb_ref, u_ref, vt_ref, x_ref, gate_ref,
                       fg_ref, o_ref, acc_ref, *, et, final_norm):
    e = pl.program_id(1)
    nk = PEER_NKEYS

    @pl.when(e == 0)
    def _():
        acc_ref[...] = jnp.zeros_like(acc_ref)

    a_t = _dot_nt(u_ref[...], h_ref[...])
    act = 0.5 * a_t * (1.0 + lax.erf(a_t * (2.0 ** -0.5)))
    blocks = []
    for ib in range(et // nk):
        i = e * (et // nk) + ib
        w = None
        for hd in range(PEER_HEADS):
            n_i = nsel_ref[hd, pl.ds(i, 1), :]
            ea_i = ea_ref[hd, pl.ds(i, 1), :]
            term = jnp.where(rank1_ref[hd] < n_i, eb_ref[hd], 0.0) * ea_i
            w = term if w is None else w + term
        blocks.append((w * act[ib * nk:(ib + 1) * nk, :]).astype(BF16))
    p_t = jnp.concatenate(blocks, axis=0)
    acc_ref[...] += _dot(vt_ref[...], p_t)

    @pl.when(e == pl.num_programs(1) - 1)
    def _():
        y = x_ref[...] + gate_ref[...] * acc_ref[...].T
        if final_norm:
            ms = jnp.mean(y * y, axis=-1, keepdims=True)
            y = y * lax.rsqrt(ms + EPS) * fg_ref[...]
        o_ref[...] = y


def _peer_dense(h_bf, tabs, u_bf, vt_bf, x, gate, final_g, td, et, rows_per_mod, final_norm):
    t, d = x.shape
    ne = u_bf.shape[0]
    r = gate.shape[1]
    nk = PEER_NKEYS
    tiles_per_mod = rows_per_mod // td
    tab_spec = pl.BlockSpec((PEER_HEADS, nk, td), lambda i, e: (0, 0, i))
    kern = functools.partial(_peer_dense_kernel, et=et, final_norm=final_norm)
    return pl.pallas_call(
        kern,
        grid=(t // td, ne // et),
        in_specs=[
            pl.BlockSpec((td, d), lambda i, e: (i, 0)),
            tab_spec, tab_spec, tab_spec, tab_spec,
            pl.BlockSpec((et, d), lambda i, e: (e, 0)),
            pl.BlockSpec((d, et), lambda i, e: (0, e)),
            pl.BlockSpec((td, d), lambda i, e: (i, 0)),
            pl.BlockSpec((None, r, d), lambda i, e: (i // tiles_per_mod, 0, 0)),
            pl.BlockSpec((1, d), lambda i, e: (0, 0)),
        ],
        out_specs=pl.BlockSpec((td, d), lambda i, e: (i, 0)),
        out_shape=jax.ShapeDtypeStruct((t, d), F32),
        scratch_shapes=[pltpu.VMEM((d, td), F32)],
        compiler_params=_cparams(("arbitrary", "arbitrary")),
        name="peer_dense",
    )(h_bf, *tabs, u_bf, vt_bf, x, gate, final_g)


def _peer_layer(x, g, shift, scale, gate, wq_t_bf, keys, u_bf, vt_bf, final_g, cand, *,
                tp, td, et, rows_per_mod, final_norm):
    h_bf, *tabs = _peer_select(x, g, shift, scale, wq_t_bf, keys, cand[0], cand[1], tp, rows_per_mod)
    return _peer_dense(h_bf, tabs, u_bf, vt_bf, x, gate, final_g, td, et, rows_per_mod, final_norm)


PEER_SAMPLE_ROWS = LANES
SAMPLE_ROWS = 16
PAST_LEN = 16384


def kernel(x_prompt, x_sample, state_pool, cache_dil_kv0, cache_dil_kv1, cache_dil_kv2, state_ret,
           c_prompt, c_sample, norm1_g, norm2_g, mod_w, mod_b, pool_w, pool_scale,
           dil_w_in, dil_w_out, ret_w_in, ret_gn_g, ret_w_out,
           peer_w_q, peer_keys, peer_u, peer_v, final_g):
    bp, seq, d = x_prompt.shape
    bs = x_sample.shape[0]
    depth = mod_w.shape[0]
    tp_rows = bp * seq
    caches = (cache_dil_kv0, cache_dil_kv1, cache_dil_kv2)
    ret_dk = d // RET_HEADS
    ret_dv = 2 * ret_dk

    sr = SAMPLE_ROWS
    n_c = bp + sr
    c_rows = -(-n_c // SUBLANES) * SUBLANES
    c_all = jnp.concatenate([c_prompt, c_sample, jnp.zeros((c_rows - bp - bs, d), F32)], axis=0)
    mods = _adaln(c_all, mod_w, mod_b).reshape(depth, c_rows, 6, d)
    pad_rows = lambda a, rows: jnp.pad(a, ((0, rows - a.shape[0]), (0, 0)))

    slope_tab = jnp.asarray(_alibi_table())
    ret_tabs = _ret_tables(ret_dk, ret_dv)
    cand = _cand_tables()
    final_g2 = final_g.reshape(1, d)

    xp = x_prompt.reshape(tp_rows, d)
    xs = pad_rows(x_sample.reshape(bs, d), sr)
    pool_p, pool_s, ret_p, ret_s = [], [], [], []
    kv_p = [[] for _ in DIL_PATTERNS]
    kv_s = [[] for _ in DIL_PATTERNS]

    for i in range(depth):
        kind, j = i % 3, i // 3
        mp = [mods[i, :bp, m].reshape(bp, 1, d) for m in range(6)]
        ms = [mods[i, bp:bp + sr, m] for m in range(6)]
        g1 = norm1_g[i].reshape(1, d)
        g2 = norm2_g[i].reshape(1, d)

        if kind == 0:
            w_bf = pool_w[j].astype(BF16)
            ps = pool_scale[j].reshape(1, d)
            xp3, st_p = _pool_layer(xp.reshape(bp, seq, d), g1, mp[0], mp[1], mp[2],
                                    jnp.zeros((bp, CARRY, d), F32), w_bf, ps,
                                    tr=512, n_real=512, pos0=0)
            xp = xp3.reshape(tp_rows, d)
            pool_p.append(st_p[:, 1:])
            xs_pad = jnp.pad(xs[:bs].reshape(bs, 1, d), ((0, 0), (0, SUBLANES - 1), (0, 0)))
            hist = jnp.pad(state_pool[j], ((0, 0), (1, 0), (0, 0)))
            xs3, st_s = _pool_layer(xs_pad, g1, ms[0][:bs].reshape(bs, 1, d), ms[1][:bs].reshape(bs, 1, d),
                                    ms[2][:bs].reshape(bs, 1, d), hist, w_bf, ps,
                                    tr=SUBLANES, n_real=1, pos0=PAST_LEN)
            xs = pad_rows(xs3[:, 0], sr)
            pool_s.append(st_s[:, 1:])
        elif kind == 1:
            w_in = dil_w_in[j].astype(BF16)
            w_out = dil_w_out[j].astype(BF16)
            hh, hd = DIL_HEADS, DIL_HEAD_DIM
            ng = len(DIL_PATTERNS)
            spg = hh * hd // SLOT

            def head_major(sl, rows):
                return sl.reshape(spg, rows, SLOT // hd, hd).transpose(1, 0, 2, 3).reshape(rows, hh, hd)

            slots = _proj_in(xp, g1, mp[0], mp[1], w_in, 1024, seq)
            o = _dil_prompt_attn(slots, slope_tab, bp, seq)
            xp = _proj_out(o, w_out, xp, mp[2], 1024, seq)
            for g, (wdw, dil) in enumerate(DIL_PATTERNS):
                nl = min(wdw, seq)
                kk = head_major(slots[(ng + g) * spg:(ng + g + 1) * spg], tp_rows).reshape(bp, seq, hh, hd)
                vv = head_major(slots[(2 * ng + g) * spg:(2 * ng + g + 1) * spg], tp_rows).reshape(bp, seq, hh, hd)
                kv_p[g].append(jnp.stack([kk[:, seq - nl:], vv[:, seq - nl:]], axis=2))

            slots_s = _proj_in(xs, g1, ms[0][None], ms[1][None], w_in, sr, sr)
            o_s = _dil_sample_attn(slots_s, [c[j] for c in caches], slope_tab)
            xs = _proj_out(pad_rows(o_s.reshape(bs, hh * hd), sr), w_out, xs, ms[2][None], sr, sr)
            for g in range(ng):
                kn = head_major(slots_s[(ng + g) * spg:(ng + g + 1) * spg], sr)[:bs]
                vn = head_major(slots_s[(2 * ng + g) * spg:(2 * ng + g + 1) * spg], sr)[:bs]
                new = jnp.stack([kn, vn], axis=1)[:, None]
                kv_s[g].append(jnp.concatenate([caches[g][j][:, 1:], new], axis=1))
        else:
            w_in = ret_w_in[j].astype(BF16)
            w_out = ret_w_out[j].astype(BF16)
            gn = ret_gn_g[j].reshape(1, RET_HEADS * ret_dv)
            slots = _proj_in(xp, g1, mp[0], mp[1], w_in, 1024, seq)
            y, st_p = _ret_prompt(slots, ret_tabs, gn, bp, seq, ret_dk, ret_dv)
            xp = _proj_out(y, w_out, xp, mp[2], 1024, seq)
            ret_p.append(st_p)
            slots_s = _proj_in(xs, g1, ms[0][None], ms[1][None], w_in, sr, sr)
            y_s, st_s = _ret_sample(slots_s, state_ret[j], ret_tabs, gn, ret_dk, ret_dv)
            xs = _proj_out(pad_rows(y_s.reshape(bs, RET_HEADS * ret_dv), sr), w_out, xs, ms[2][None], sr, sr)
            ret_s.append(st_s)

        last = i == depth - 1
        wq_t = peer_w_q[i].T.astype(BF16)
        u_bf = peer_u[i].astype(BF16)
        vt_bf = peer_v[i].T.astype(BF16)
        xp = _peer_layer(xp, g2, mp[3], mp[4], mp[5], wq_t, peer_keys[i], u_bf, vt_bf, final_g2, cand,
                         tp=256, td=512, et=512, rows_per_mod=seq, final_norm=last)
        padr = lambda a: pad_rows(a, PEER_SAMPLE_ROWS)
        xs_new = _peer_layer(padr(xs), g2, padr(ms[3])[None], padr(ms[4])[None], padr(ms[5])[None],
                             wq_t, peer_keys[i], u_bf, vt_bf, final_g2, cand,
                             tp=PEER_SAMPLE_ROWS, td=PEER_SAMPLE_ROWS, et=512,
                             rows_per_mod=PEER_SAMPLE_ROWS, final_norm=last)
        xs = xs_new[:sr]

    y_prompt = xp.reshape(bp, seq, d)
    y_sample = xs[:bs].reshape(bs, 1, d)
    return (y_prompt, y_sample,
            jnp.stack(pool_p), jnp.stack(pool_s),
            jnp.stack(kv_p[0]), jnp.stack(kv_s[0]),
            jnp.stack(kv_p[1]), jnp.stack(kv_s[1]),
            jnp.stack(kv_p[2]), jnp.stack(kv_s[2]),
            jnp.stack(ret_p), jnp.stack(ret_s))
```

```python
import functools
import math

import jax
import jax.numpy as jnp
import numpy as np
from jax import lax
from jax.experimental import pallas as pl
from jax.experimental.pallas import tpu as pltpu

F32 = jnp.float32
BF16 = jnp.bfloat16
EPS = 1e-6
NEG_INF = float("-inf")

LANES = 128
SUBLANES = 8
BF16_ROWS = 16
VMEM_LIMIT_BYTES = 56 * 1024 * 1024

POOL_WINDOWS = (2, 4, 8, 16)
POOL_HIST = 15
DIL_PATTERNS = ((128, 1), (512, 4), (2048, 16))
DIL_HEADS = 8
DIL_HEAD_DIM = 128
DIL_BLOCK = 128
RET_HEADS = 8
RET_CHUNK = 128
PEER_HEADS = 8
PEER_NKEYS = 128
PEER_TOPK = 16
SLOT = 1024


def _cparams(sem, vmem=VMEM_LIMIT_BYTES):
    return pltpu.CompilerParams(dimension_semantics=sem, vmem_limit_bytes=vmem)


def _rms_mod(x, g, shift, scale):
    ms = jnp.mean(x * x, axis=-1, keepdims=True)
    y = x * lax.rsqrt(ms + EPS) * g
    return y * (1.0 + scale) + shift


def _dot(a, b):
    return jnp.dot(a, b, preferred_element_type=F32)


def _dot_nt(a, b):
    return lax.dot_general(a, b, (((1,), (1,)), ((), ())), preferred_element_type=F32)


def _dot_tn(a, b):
    return lax.dot_general(a, b, (((0,), (0,)), ((), ())), preferred_element_type=F32)


def _adaln_kernel(c_ref, w_ref, b_ref, o_ref):
    c = c_ref[...]
    a = c * (1.0 / (1.0 + jnp.exp(-c)))
    a_hi = a.astype(BF16)
    a_lo = (a - a_hi.astype(F32)).astype(BF16)
    w = w_ref[...]
    w_hi = w.astype(BF16)
    w_lo = (w - w_hi.astype(F32)).astype(BF16)
    acc = _dot(a_hi, w_hi) + _dot(a_hi, w_lo) + _dot(a_lo, w_hi)
    o_ref[...] = acc + b_ref[...]


def _adaln(c_all, mod_w, mod_b, tn=1024):
    depth, d, n = mod_w.shape
    rows = c_all.shape[0]
    return pl.pallas_call(
        _adaln_kernel,
        grid=(depth, n // tn),
        in_specs=[
            pl.BlockSpec((rows, d), lambda l, j: (0, 0)),
            pl.BlockSpec((None, d, tn), lambda l, j: (l, 0, j)),
            pl.BlockSpec((None, 1, tn), lambda l, j: (l, 0, j)),
        ],
        out_specs=pl.BlockSpec((None, rows, tn), lambda l, j: (l, 0, j)),
        out_shape=jax.ShapeDtypeStruct((depth, rows, n), F32),
        compiler_params=_cparams(("arbitrary", "arbitrary")),
        name="adaln",
    )(c_all, mod_w, mod_b.reshape(depth, 1, n))


def _proj_in_kernel(x_ref, g_ref, sh_ref, sc_ref, w_ref, o_ref, h_ref):
    @pl.when(pl.program_id(1) == 0)
    def _():
        h_ref[...] = _rms_mod(x_ref[...], g_ref[...], sh_ref[...], sc_ref[...]).astype(BF16)

    o_ref[...] = _dot(h_ref[...], w_ref[...])


def _proj_in(x, g, shift, scale, w_bf, tm, rows_per_mod):
    t, d = x.shape
    n = w_bf.shape[1]
    r = shift.shape[1]
    tiles_per_mod = rows_per_mod // tm
    mod_spec = pl.BlockSpec((None, r, d), lambda i, j: (i // tiles_per_mod, 0, 0))
    return pl.pallas_call(
        _proj_in_kernel,
        grid=(t // tm, n // SLOT),
        in_specs=[
            pl.BlockSpec((tm, d), lambda i, j: (i, 0)),
            pl.BlockSpec((1, d), lambda i, j: (0, 0)),
            mod_spec,
            mod_spec,
            pl.BlockSpec((d, SLOT), lambda i, j: (0, j)),
        ],
        out_specs=pl.BlockSpec((None, tm, SLOT), lambda i, j: (j, i, 0)),
        out_shape=jax.ShapeDtypeStruct((n // SLOT, t, SLOT), F32),
        scratch_shapes=[pltpu.VMEM((tm, d), BF16)],
        compiler_params=_cparams(("arbitrary", "arbitrary")),
        name="proj_in",
    )(x, g, shift, scale, w_bf)


def _proj_out_kernel(a_ref, w_ref, x_ref, gate_ref, o_ref):
    y = _dot(a_ref[...].astype(BF16), w_ref[...])
    o_ref[...] = x_ref[...] + gate_ref[...] * y


def _proj_out(a, w_bf, x, gate, tm, rows_per_mod, tn=512):
    t, k = a.shape
    d = w_bf.shape[1]
    r = gate.shape[1]
    tiles_per_mod = rows_per_mod // tm
    return pl.pallas_call(
        _proj_out_kernel,
        grid=(t // tm, d // tn),
        in_specs=[
            pl.BlockSpec((tm, k), lambda i, j: (i, 0)),
            pl.BlockSpec((k, tn), lambda i, j: (0, j)),
            pl.BlockSpec((tm, tn), lambda i, j: (i, j)),
            pl.BlockSpec((None, r, tn), lambda i, j: (i // tiles_per_mod, 0, j)),
        ],
        out_specs=pl.BlockSpec((tm, tn), lambda i, j: (i, j)),
        out_shape=jax.ShapeDtypeStruct((t, d), F32),
        compiler_params=_cparams(("arbitrary", "arbitrary")),
        name="proj_out",
    )(a, w_bf, x, gate)


CARRY = 16


def _pool_kernel(x_ref, g_ref, sh_ref, sc_ref, gate_ref, hist_ref, w_ref, ps_ref,
                 o_ref, st_ref, ext_ref, *, tr, n_real, pos0):
    t = pl.program_id(1)
    d = x_ref.shape[-1]
    grp = d // len(POOL_WINDOWS)

    @pl.when(t == 0)
    def _():
        ext_ref[0:CARRY, :] = hist_ref[...]

    x = x_ref[...]
    h = _rms_mod(x, g_ref[...], sh_ref[...], sc_ref[...])
    ext_ref[CARRY:CARRY + tr, :] = h

    pos = (pos0 + t * tr + lax.broadcasted_iota(jnp.int32, (tr, 1), 0)).astype(F32)
    ys = []
    for gi, wdw in enumerate(POOL_WINDOWS):
        c0 = gi * grp
        cur = ext_ref[CARRY:CARRY + tr, c0:c0 + grp]
        acc = cur
        for k in range(1, wdw):
            acc = acc + ext_ref[CARRY - k:CARRY - k + tr, c0:c0 + grp]
        cnt = jnp.minimum(float(wdw), pos + 1.0)
        dd = acc / cnt - cur
        ys.append(_dot(dd.astype(BF16), w_ref[gi]))
    y = jnp.concatenate(ys, axis=-1) * ps_ref[...]
    o_ref[...] = x + gate_ref[...] * y

    st_ref[...] = ext_ref[n_real:n_real + CARRY, :]
    ext_ref[0:CARRY, :] = ext_ref[tr:tr + CARRY, :]


def _pool_layer(x, g, shift, scale, gate, hist16, w_bf, pscale, *, tr, n_real, pos0):
    b, s, d = x.shape
    grp = d // len(POOL_WINDOWS)
    mod_spec = pl.BlockSpec((None, 1, d), lambda i, t: (i, 0, 0))
    kern = functools.partial(_pool_kernel, tr=tr, n_real=n_real, pos0=pos0)
    return pl.pallas_call(
        kern,
        grid=(b, s // tr),
        in_specs=[
            pl.BlockSpec((None, tr, d), lambda i, t: (i, t, 0)),
            pl.BlockSpec((1, d), lambda i, t: (0, 0)),
            mod_spec, mod_spec, mod_spec,
            pl.BlockSpec((None, CARRY, d), lambda i, t: (i, 0, 0)),
            pl.BlockSpec((len(POOL_WINDOWS), grp, grp), lambda i, t: (0, 0, 0)),
            pl.BlockSpec((1, d), lambda i, t: (0, 0)),
        ],
        out_specs=[
            pl.BlockSpec((None, tr, d), lambda i, t: (i, t, 0)),
            pl.BlockSpec((None, CARRY, d), lambda i, t: (i, 0, 0)),
        ],
        out_shape=[
            jax.ShapeDtypeStruct((b, s, d), F32),
            jax.ShapeDtypeStruct((b, CARRY, d), F32),
        ],
        scratch_shapes=[pltpu.VMEM((CARRY + tr, d), F32)],
        compiler_params=_cparams(("arbitrary", "arbitrary")),
        name="pool",
    )(x, g, shift, scale, gate, hist16, w_bf, pscale)


def _alibi_table():
    n = len(DIL_PATTERNS) * DIL_HEADS
    slopes = np.exp2(-8.0 * np.arange(1, n + 1, dtype=np.float64) / n).reshape(len(DIL_PATTERNS), DIL_HEADS)
    dil = np.array([p[1] for p in DIL_PATTERNS], np.float64)[:, None]
    tab = (slopes * dil).T
    return np.broadcast_to(tab[:, :, None], (DIL_HEADS, len(DIL_PATTERNS), LANES)).astype(np.float32)


def _dil_prompt_kernel(*refs, seq):
    q_refs = refs[0:3]
    k_refs = refs[3:6]
    v_refs = refs[6:9]
    slope_ref = refs[9]
    o_ref = refs[10]
    qd, kd, vd, og, lg = refs[11:16]
    oi = refs[16:19]
    li = refs[19:22]
    blk = DIL_BLOCK
    hd = DIL_HEAD_DIM
    scale = hd ** -0.5

    row = lax.broadcasted_iota(jnp.int32, (blk, blk), 0)
    col = lax.broadcasted_iota(jnp.int32, (blk, blk), 1)
    dist_c = (row - col).astype(F32)
    mask_c = row >= col
    dist_p = dist_c + float(blk)
    mask_p = col >= row

    for g, (wdw, dil) in enumerate(DIL_PATTERNS):
        ls = seq // dil
        nb = ls // blk
        if dil == 1:
            qs, ks, vs = q_refs[g], k_refs[g], v_refs[g]
        else:
            for res in range(dil):
                qd[res * ls:(res + 1) * ls, :] = q_refs[g][pl.ds(res, ls, stride=dil), :]
                kd[res * ls:(res + 1) * ls, :] = k_refs[g][pl.ds(res, ls, stride=dil), :]
                vd[res * ls:(res + 1) * ls, :] = v_refs[g][pl.ds(res, ls, stride=dil), :]
            qs, ks, vs = qd, kd, vd
        slope = slope_ref[g:g + 1, :]
        bias_c = slope * dist_c
        bias_p = slope * dist_p

        def step(it, carry, qs=qs, ks=ks, vs=vs, nb=nb, bias_c=bias_c, bias_p=bias_p):
            r0 = pl.multiple_of(it * blk, blk)
            has_prev = (it % nb) != 0
            rp = pl.multiple_of(jnp.maximum(it - 1, 0) * blk, blk)
            qb = qs[pl.ds(r0, blk), :].astype(BF16)
            kc = ks[pl.ds(r0, blk), :].astype(BF16)
            kp = ks[pl.ds(rp, blk), :].astype(BF16)
            s_c = _dot_nt(qb, kc) * scale - bias_c
            s_p = _dot_nt(qb, kp) * scale - bias_p
            pen = jnp.where(has_prev, 0.0, NEG_INF)
            s_c = jnp.where(mask_c, s_c, NEG_INF)
            s_p = jnp.where(mask_p, s_p + pen, NEG_INF)
            m = jnp.maximum(jnp.max(s_c, axis=-1, keepdims=True), jnp.max(s_p, axis=-1, keepdims=True))
            p_c = jnp.exp(s_c - m)
            p_p = jnp.exp(s_p - m)
            den = jnp.sum(p_c, axis=-1, keepdims=True) + jnp.sum(p_p, axis=-1, keepdims=True)
            vc = vs[pl.ds(r0, blk), :].astype(BF16)
            vp = vs[pl.ds(rp, blk), :].astype(BF16)
            o = (_dot(p_c.astype(BF16), vc) + _dot(p_p.astype(BF16), vp)) / den
            og[pl.ds(r0, blk), :] = o
            lg[pl.ds(r0, blk), :] = jnp.broadcast_to(m + jnp.log(den), (blk, hd))
            return carry

        lax.fori_loop(0, seq // blk, step, 0, unroll=4)

        if dil == 1:
            oi[g][...] = og[...]
            li[g][...] = lg[...]
        else:
            for res in range(dil):
                oi[g][pl.ds(res, ls, stride=dil), :] = og[res * ls:(res + 1) * ls, :]
                li[g][pl.ds(res, ls, stride=dil), :] = lg[res * ls:(res + 1) * ls, :]

    l0, l1, l2 = li[0][...], li[1][...], li[2][...]
    mx = jnp.maximum(jnp.maximum(l0, l1), l2)
    e0, e1, e2 = jnp.exp(l0 - mx), jnp.exp(l1 - mx), jnp.exp(l2 - mx)
    o_ref[...] = (e0 * oi[0][...] + e1 * oi[1][...] + e2 * oi[2][...]) / (e0 + e1 + e2)


def _dil_prompt_attn(slots, slope_tab, batch, seq):
    hd = DIL_HEAD_DIM
    per_slot = SLOT // hd
    ng = len(DIL_PATTERNS)
    slots_per_group = (DIL_HEADS * hd) // SLOT

    def spec(kind, g):
        base = (kind * ng + g) * slots_per_group
        return pl.BlockSpec((None, seq, hd), lambda n, h: (base + h // per_slot, n, h % per_slot))

    in_specs = [spec(kind, g) for kind in range(3) for g in range(ng)]
    in_specs.append(pl.BlockSpec((None, ng, LANES), lambda n, h: (h, 0, 0)))
    kern = functools.partial(_dil_prompt_kernel, seq=seq)
    return pl.pallas_call(
        kern,
        grid=(batch, DIL_HEADS),
        in_specs=in_specs,
        out_specs=pl.BlockSpec((seq, hd), lambda n, h: (n, h)),
        out_shape=jax.ShapeDtypeStruct((batch * seq, DIL_HEADS * hd), F32),
        scratch_shapes=[pltpu.VMEM((seq, hd), F32) for _ in range(5 + 2 * ng)],
        compiler_params=_cparams(("arbitrary", "arbitrary")),
        name="dil_attn_prompt",
    )(*([slots] * 9), slope_tab)


def _dil_sample_kernel(slots_ref, c0_ref, c1_ref, c2_ref, slope_ref, o_ref):
    n = pl.program_id(0)
    hd = DIL_HEAD_DIM
    per_slot = SLOT // hd
    ng = len(DIL_PATTERNS)
    spg = (DIL_HEADS * hd) // SLOT
    scale = hd ** -0.5
    caches = (c0_ref, c1_ref, c2_ref)
    nk = DIL_BLOCK
    steps = (nk - lax.broadcasted_iota(jnp.int32, (nk, 1), 0)).astype(F32)

    outs = []
    for h in range(DIL_HEADS):
        sub = (h % per_slot) * hd

        def row(kind, g):
            full = slots_ref[(kind * ng + g) * spg + h // per_slot, pl.ds(n, 1), :]
            return full[:, sub:sub + hd]

        o_g, l_g = [], []
        for g in range(ng):
            qh, kn, vn = row(0, g), row(1, g), row(2, g)
            kc = caches[g][:, h * hd:(h + 1) * hd]
            vc = caches[g][:, DIL_HEADS * hd + h * hd:DIL_HEADS * hd + (h + 1) * hd]
            slope = slope_ref[h, g:g + 1, 0:1]
            s_c = jnp.sum(kc * qh, axis=-1, keepdims=True) * scale - slope * steps
            s_n = jnp.sum(kn * qh, axis=-1, keepdims=True) * scale
            m = jnp.maximum(jnp.max(s_c, axis=0, keepdims=True), s_n)
            p_c = jnp.exp(s_c - m)
            p_n = jnp.exp(s_n - m)
            den = jnp.sum(p_c, axis=0, keepdims=True) + p_n
            o = (jnp.sum(p_c * vc, axis=0, keepdims=True) + p_n * vn) / den
            o_g.append(o)
            l_g.append(m + jnp.log(den))
        mx = jnp.maximum(jnp.maximum(l_g[0], l_g[1]), l_g[2])
        e = [jnp.exp(l - mx) for l in l_g]
        outs.append((e[0] * o_g[0] + e[1] * o_g[1] + e[2] * o_g[2]) / (e[0] + e[1] + e[2]))
    o_ref[...] = jnp.concatenate(outs, axis=-1)


def _dil_sample_attn(slots, caches, slope_tab):
    b = caches[0].shape[0]
    width = 2 * DIL_HEADS * DIL_HEAD_DIM
    ins = []
    specs = [pl.BlockSpec(slots.shape, lambda n: (0, 0, 0))]
    for g, (wdw, dil) in enumerate(DIL_PATTERNS):
        ins.append(caches[g].reshape(b, wdw // dil, dil * width))
        specs.append(pl.BlockSpec((None, wdw // dil, width), lambda n: (n, 0, 0)))
    specs.append(pl.BlockSpec(slope_tab.shape, lambda n: (0, 0, 0)))
    return pl.pallas_call(
        _dil_sample_kernel,
        grid=(b,),
        in_specs=specs,
        out_specs=pl.BlockSpec((None, 1, DIL_HEADS * DIL_HEAD_DIM), lambda n: (n, 0, 0)),
        out_shape=jax.ShapeDtypeStruct((b, 1, DIL_HEADS * DIL_HEAD_DIM), F32),
        compiler_params=_cparams(("arbitrary",)),
        name="dil_attn_sample",
    )(slots, *ins, slope_tab)


def _ret_tables(dk, dv):
    c = RET_CHUNK
    log_g = np.log1p(-np.exp2(-5.0 - np.arange(RET_HEADS, dtype=np.float64)))
    pos = np.arange(c, dtype=np.float64)
    diff = pos[:, None] - pos[None, :]
    dmask = np.where(diff >= 0, np.exp(log_g[:, None, None] * np.maximum(diff, 0.0)), 0.0)
    q_dec = np.exp(log_g[:, None] * (pos[None, :] + 1.0))
    k_dec = np.exp(log_g[:, None] * (c - 1.0 - pos[None, :]))
    c_dec = np.exp(log_g * c)
    gam = np.exp(log_g)
    f = lambda a: jnp.asarray(a.astype(np.float32))
    return dict(
        dmask=f(dmask),
        q_dec=f(np.broadcast_to(q_dec[:, :, None], (RET_HEADS, c, dk))),
        k_dec=f(np.broadcast_to(k_dec[:, :, None], (RET_HEADS, c, dk))),
        c_dec=f(np.broadcast_to(c_dec[:, None, None], (RET_HEADS, 1, LANES))),
        gamma=f(np.broadcast_to(gam[:, None, None], (RET_HEADS, 1, LANES))),
    )


def _gn_gate(o, gn, gate):
    mu = jnp.mean(o, axis=-1, keepdims=True)
    cen = o - mu
    var = jnp.mean(cen * cen, axis=-1, keepdims=True)
    y = cen * lax.rsqrt(var + EPS) * gn
    return gate * (1.0 / (1.0 + jnp.exp(-gate))) * y


def _ret_prompt_kernel(q_ref, k_ref, v_ref, gt_ref, dm_ref, qd_ref, kd_ref, cd_ref, gn_ref,
                       y_ref, st_ref, state_ref, *, seq):
    c = RET_CHUNK
    dk = q_ref.shape[-1]
    state_ref[...] = jnp.zeros_like(state_ref)
    dmask = dm_ref[...]
    q_dec = qd_ref[...]
    k_dec = kd_ref[...]
    c_dec = cd_ref[:, 0:1]
    gn = gn_ref[...]
    kscale = dk ** -0.5

    def step(ci, carry):
        r0 = pl.multiple_of(ci * c, c)
        qc = q_ref[pl.ds(r0, c), :]
        kc = k_ref[pl.ds(r0, c), :] * kscale
        vc = v_ref[pl.ds(r0, c), :].astype(BF16)
        state = state_ref[...]
        att = _dot_nt(qc.astype(BF16), kc.astype(BF16)) * dmask
        o = _dot(att.astype(BF16), vc) + _dot((qc * q_dec).astype(BF16), state.astype(BF16))
        state_ref[...] = c_dec * state + _dot_tn((kc * k_dec).astype(BF16), vc)
        y_ref[pl.ds(r0, c), :] = _gn_gate(o, gn, gt_ref[pl.ds(r0, c), :]).astype(y_ref.dtype)
        return carry

    lax.fori_loop(0, seq // c, step, 0)
    st_ref[...] = state_ref[...]


def _ret_prompt(slots, tabs, gn, batch, seq, dk, dv):
    hq = SLOT // dk
    nq = RET_HEADS // hq
    hv = SLOT // dv
    nv = RET_HEADS // hv
    kern = functools.partial(_ret_prompt_kernel, seq=seq)
    return pl.pallas_call(
        kern,
        grid=(batch, RET_HEADS),
        in_specs=[
            pl.BlockSpec((None, seq, dk), lambda n, h: (h // hq, n, h % hq)),
            pl.BlockSpec((None, seq, dk), lambda n, h: (nq + h // hq, n, h % hq)),
            pl.BlockSpec((None, seq, dv), lambda n, h: (2 * nq + h // hv, n, h % hv)),
            pl.BlockSpec((None, seq, dv), lambda n, h: (2 * nq + nv + h // hv, n, h % hv)),
            pl.BlockSpec((None, RET_CHUNK, RET_CHUNK), lambda n, h: (h, 0, 0)),
            pl.BlockSpec((None, RET_CHUNK, dk), lambda n, h: (h, 0, 0)),
            pl.BlockSpec((None, RET_CHUNK, dk), lambda n, h: (h, 0, 0)),
            pl.BlockSpec((None, 1, LANES), lambda n, h: (h, 0, 0)),
            pl.BlockSpec((1, dv), lambda n, h: (0, h)),
        ],
        out_specs=[
            pl.BlockSpec((seq, dv), lambda n, h: (n, h)),
            pl.BlockSpec((None, None, dk, dv), lambda n, h: (n, h, 0, 0)),
        ],
        out_shape=[
            jax.ShapeDtypeStruct((batch * seq, RET_HEADS * dv), BF16),
            jax.ShapeDtypeStruct((batch, RET_HEADS, dk, dv), F32),
        ],
        scratch_shapes=[pltpu.VMEM((dk, dv), F32)],
        compiler_params=_cparams(("arbitrary", "arbitrary")),
        name="ret_prompt",
    )(slots, slots, slots, slots, tabs["dmask"], tabs["q_dec"], tabs["k_dec"], tabs["c_dec"], gn)


def _ret_sample_kernel(slots_ref, s0_ref, gam_ref, gn_ref, y_ref, st_ref, *, dk, dv):
    n = pl.program_id(0)
    h = pl.program_id(1)
    hq = SLOT // dk
    nq = RET_HEADS // hq
    hv = SLOT // dv
    nv = RET_HEADS // hv
    gam = gam_ref[:, 0:1]

    def head_row(base, per_slot, width):
        full = slots_ref[base + h // per_slot, pl.ds(n, 1), :]
        parts = [full[:, i * width:(i + 1) * width] for i in range(per_slot)]
        out = parts[0]
        for i in range(1, per_slot):
            out = jnp.where(h % per_slot == i, parts[i], out)
        return out

    q = head_row(0, hq, dk)
    k = head_row(nq, hq, dk) * (dk ** -0.5)
    v = head_row(2 * nq, hv, dv)
    gate = head_row(2 * nq + nv, hv, dv)

    eye = lax.broadcasted_iota(jnp.int32, (dk, dk), 0) == lax.broadcasted_iota(jnp.int32, (dk, dk), 1)
    q_col = jnp.sum(jnp.where(eye, q, 0.0), axis=-1, keepdims=True)
    k_col = jnp.sum(jnp.where(eye, k, 0.0), axis=-1, keepdims=True)
    state = s0_ref[...]
    qk = jnp.sum(q * k, axis=-1, keepdims=True)
    o = qk * v + gam * jnp.sum(q_col * state, axis=0, keepdims=True)
    st_ref[...] = gam * state + k_col * v
    y_ref[...] = _gn_gate(o, gn_ref[...], gate)


def _ret_sample(slots, s0, tabs, gn, dk, dv):
    b = s0.shape[0]
    kern = functools.partial(_ret_sample_kernel, dk=dk, dv=dv)
    return pl.pallas_call(
        kern,
        grid=(b, RET_HEADS),
        in_specs=[
            pl.BlockSpec(slots.shape, lambda n, h: (0, 0, 0)),
            pl.BlockSpec((None, None, dk, dv), lambda n, h: (n, h, 0, 0)),
            pl.BlockSpec((None, 1, LANES), lambda n, h: (h, 0, 0)),
            pl.BlockSpec((1, dv), lambda n, h: (0, h)),
        ],
        out_specs=[
            pl.BlockSpec((None, 1, dv), lambda n, h: (n, 0, h)),
            pl.BlockSpec((None, None, dk, dv), lambda n, h: (n, h, 0, 0)),
        ],
        out_shape=[
            jax.ShapeDtypeStruct((b, 1, RET_HEADS * dv), F32),
            jax.ShapeDtypeStruct((b, RET_HEADS, dk, dv), F32),
        ],
        compiler_params=_cparams(("arbitrary", "arbitrary")),
        name="ret_sample",
    )(slots, s0, tabs["gamma"], gn)


def _cand_tables():
    k = PEER_TOPK
    rows, cols = [], []
    rows += [0] * k
    cols += list(range(k))
    for r in range(1, SUBLANES):
        rows += [r] * SUBLANES
        cols += list(range(SUBLANES))
    rows += list(range(SUBLANES, k))
    cols += [0] * (k - SUBLANES)
    rows, cols = np.array(rows), np.array(cols)
    valid = (rows + 1) * (cols + 1) <= k
    pos = (rows * k + cols).astype(np.float32)
    bias = np.where(valid, 0.0, CAND_PAD).astype(np.float32)
    f = lambda a: jnp.asarray(np.broadcast_to(a[:, None], (a.shape[0], LANES)).copy())
    return f(pos), f(bias)


MARK = float(2 ** 100)
CAND_PAD = -float(2 ** 90)


def _marker(r):
    return -MARK * (1.0 + r / 64.0)


def _pack_bf16(x):
    return pltpu.bitcast(x.astype(BF16), jnp.uint32)


def _unpack_bf16(x):
    return pltpu.bitcast(x, BF16)


def _stack_rows(rows):
    n = len(rows)
    idx = lax.broadcasted_iota(jnp.int32, (n, rows[0].shape[-1]), 0)
    out = jnp.broadcast_to(rows[n - 1], idx.shape)
    for i in range(n - 2, -1, -1):
        out = jnp.where(idx == i, rows[i], out)
    return out


def _extract(z, k, pos):
    vals = []
    for r in range(k):
        m = jnp.max(z, axis=0, keepdims=True)
        hit = z == m
        if pos is not None:
            first = jnp.min(jnp.where(hit, pos, MARK), axis=0, keepdims=True)
            hit = pos == first
        z = jnp.where(hit, _marker(r), z)
        vals.append(m)
    return z, vals


def _count_marked(z):
    return jnp.sum(jnp.where(z <= -MARK, 1.0, 0.0), axis=0, keepdims=True)


def _select_block(s0, s1, key_pos, cpos, cbias, exact_ties):
    k = PEER_TOPK
    z0, a = _extract(s0, k, key_pos if exact_ties else None)
    z1, b = _extract(s1, k, key_pos if exact_ties else None)
    a_hi = _stack_rows(a[SUBLANES:])
    b_all = _stack_rows(b)
    b_lo = b_all[0:SUBLANES]
    cand = [a[0] + b_all] + [a[r] + b_lo for r in range(1, SUBLANES)] + [a_hi + b[0]]
    cand = jnp.concatenate(cand, axis=0) + cbias
    zc, _ = _extract(cand, k, cpos if exact_ties else None)
    sel = jnp.where(zc <= -MARK, 1.0, 0.0)
    top = a[0] + b[0]
    zsum = jnp.sum(sel * jnp.exp(cand - top), axis=0, keepdims=True)
    n_r = [jnp.sum(sel[0:k], axis=0, keepdims=True)]
    rows_kept = sel[0:1]
    for r in range(1, SUBLANES):
        lo = k + (r - 1) * SUBLANES
        n_r.append(jnp.sum(sel[lo:lo + SUBLANES], axis=0, keepdims=True))
        rows_kept = rows_kept + sel[lo:lo + 1]
    lo = k + (SUBLANES - 1) * SUBLANES
    rows_kept = rows_kept + jnp.sum(sel[lo:lo + SUBLANES], axis=0, keepdims=True)
    nsel = jnp.zeros(s0.shape, F32)
    for r in range(SUBLANES):
        nsel = jnp.where(z0 == _marker(r), n_r[r], nsel)
    last_kept = -MARK * (1.0 + rows_kept * (1.0 / 64.0))
    nsel = jnp.where(z0 <= _marker(SUBLANES), jnp.where(z0 > last_kept, 1.0, 0.0), nsel)
    rank1 = jnp.where(z1 <= -MARK, (z1 * (-1.0 / MARK) - 1.0) * 64.0, float(k))
    ea = jnp.exp(s0 - a[0]) / zsum
    eb = jnp.exp(s1 - b[0])
    ties = (jnp.abs(_count_marked(z0) - k) + jnp.abs(_count_marked(z1) - k)
            + jnp.abs(jnp.sum(sel, axis=0, keepdims=True) - k))
    return rank1, nsel, ea, eb, ties


def _peer_select_kernel(x_ref, g_ref, sh_ref, sc_ref, wq_ref, keys_ref, cpos_ref, cbias_ref,
                        h_ref, rank1_ref, nsel_ref, ea_ref, eb_ref, s_ref, *, tp):
    nk = PEER_NKEYS
    h = _rms_mod(x_ref[...], g_ref[...], sh_ref[...], sc_ref[...])
    hb = h.astype(BF16)
    h_ref[...] = hb
    q_t = _dot_nt(wq_ref[...], hb)
    for hp in range(2 * PEER_HEADS):
        key = keys_ref[hp % 2].astype(BF16)
        s_ref[hp] = _dot(key, q_t[hp * nk:(hp + 1) * nk, :].astype(BF16))

    key_pos = lax.broadcasted_iota(jnp.int32, (nk, LANES), 0).astype(F32)

    def body(it, carry):
        head = it // (tp // LANES)
        lg = it % (tp // LANES)
        l0 = pl.multiple_of(lg * LANES, LANES)

        def run(exact_ties):
            s0 = s_ref[2 * head, :, pl.ds(l0, LANES)]
            s1 = s_ref[2 * head + 1, :, pl.ds(l0, LANES)]
            rank1, nsel, ea, eb, ties = _select_block(s0, s1, key_pos, cpos_ref[...], cbias_ref[...], exact_ties)
            rank1_ref[head, :, pl.ds(l0, LANES)] = _pack_bf16(rank1)
            nsel_ref[head, :, pl.ds(l0, LANES)] = nsel
            ea_ref[head, :, pl.ds(l0, LANES)] = ea
            eb_ref[head, :, pl.ds(l0, LANES)] = _pack_bf16(eb)
            return ties

        ties = run(False)

        @pl.when(jnp.max(ties) > 0.0)
        def _():
            run(True)

        return carry

    lax.fori_loop(0, PEER_HEADS * (tp // LANES), body, 0)


def _peer_select(x, g, shift, scale, wq_t_bf, keys, cpos, cbias, tp, rows_per_mod):
    t, d = x.shape
    r = shift.shape[1]
    tiles_per_mod = rows_per_mod // tp
    nk = PEER_NKEYS
    mod_spec = pl.BlockSpec((None, r, d), lambda i: (i // tiles_per_mod, 0, 0))
    row_spec = pl.BlockSpec((PEER_HEADS, nk, tp), lambda i: (0, 0, i))
    row_tab = jax.ShapeDtypeStruct((PEER_HEADS, nk, t), F32)
    pk_spec = pl.BlockSpec((PEER_HEADS, nk // 2, tp), lambda i: (0, 0, i))
    pk_tab = jax.ShapeDtypeStruct((PEER_HEADS, nk // 2, t), jnp.uint32)
    kern = functools.partial(_peer_select_kernel, tp=tp)
    return pl.pallas_call(
        kern,
        grid=(t // tp,),
        in_specs=[
            pl.BlockSpec((tp, d), lambda i: (i, 0)),
            pl.BlockSpec((1, d), lambda i: (0, 0)),
            mod_spec, mod_spec,
            pl.BlockSpec(wq_t_bf.shape, lambda i: (0, 0)),
            pl.BlockSpec(keys.shape, lambda i: (0, 0, 0)),
            pl.BlockSpec(cpos.shape, lambda i: (0, 0)),
            pl.BlockSpec(cbias.shape, lambda i: (0, 0)),
        ],
        out_specs=[pl.BlockSpec((tp, d), lambda i: (i, 0)), pk_spec, row_spec, row_spec, pk_spec],
        out_shape=[jax.ShapeDtypeStruct((t, d), BF16), pk_tab, row_tab, row_tab, pk_tab],
        scratch_shapes=[pltpu.VMEM((2 * PEER_HEADS, nk, tp), F32)],
        compiler_params=_cparams(("arbitrary",)),
        name="peer_select",
    )(x, g, shift, scale, wq_t_bf, keys, cpos, cbias)


def _peer_dense_kernel(h_ref, rank1_ref, nsel_ref, ea_ref, eb_ref, u_ref, v_ref, x_ref, gate_ref,
                       fg_ref, o_ref, acc_ref, a0_ref, a1_ref, p0_ref, p1_ref, *, et, td, final_norm):
    e = pl.program_id(1)
    last = pl.num_programs(1) - 1
    nk = PEER_NKEYS
    half = et // 2
    nb = half // nk
    n_half_tiles = 2 * last

    @pl.when(e == 0)
    def _():
        acc_ref[...] = jnp.zeros_like(acc_ref)
        a1_ref[...] = jnp.zeros_like(a1_ref)
        p0_ref[...] = jnp.zeros_like(p0_ref)

    def consume(a_ref, p_ref, tile, live):
        kt = nk // BF16_ROWS
        tile3 = (kt, BF16_ROWS, LANES)
        for ib in range(nb):
            i = tile * nb + ib
            n_rows = [nsel_ref[hd, pl.ds(i, 1), :] for hd in range(PEER_HEADS)]
            ea_rows = [ea_ref[hd, pl.ds(i, 1), :] * live for hd in range(PEER_HEADS)]
            for lg in range(td // LANES):
                ls = slice(lg * LANES, (lg + 1) * LANES)
                a = a_ref[ib * nk:(ib + 1) * nk, ls]
                act = (0.5 * a * (1.0 + lax.erf(a * (2.0 ** -0.5)))).astype(BF16).reshape(tile3)
                w = None
                for hd in range(PEER_HEADS):
                    n_b = jnp.broadcast_to(n_rows[hd][:, ls], (BF16_ROWS, LANES)).astype(BF16)
                    ea_b = jnp.broadcast_to(ea_rows[hd][:, ls], (BF16_ROWS, LANES)).astype(BF16)
                    r1 = _unpack_bf16(rank1_ref[hd, :, ls]).reshape(tile3)
                    eb = _unpack_bf16(eb_ref[hd, :, ls]).reshape(tile3)
                    term = jnp.where(r1 < n_b, eb, 0) * ea_b
                    w = term if w is None else w + term
                p_ref[ib * (nk // 2):(ib + 1) * (nk // 2), ls] = pltpu.bitcast((w * act).reshape(nk, LANES), jnp.uint32)

    h = h_ref[...]
    a0_ref[...] = _dot_nt(u_ref[0:half, :], h)
    d_even = _dot_tn(_unpack_bf16(p0_ref[...]), v_ref[0:half, :])
    consume(a1_ref, p1_ref, jnp.maximum(2 * e - 1, 0), (e > 0).astype(F32))
    a1_ref[...] = _dot_nt(u_ref[half:et, :], h)
    consume(a0_ref, p0_ref, jnp.minimum(2 * e, n_half_tiles - 1), (e < last).astype(F32))
    d_odd = _dot_tn(_unpack_bf16(p1_ref[...]), v_ref[half:et, :])
    acc_ref[...] += d_even + d_odd

    @pl.when(e == last)
    def _():
        y = x_ref[...] + gate_ref[...] * acc_ref[...]
        if final_norm:
            ms = jnp.mean(y * y, axis=-1, keepdims=True)
            y = y * lax.rsqrt(ms + EPS) * fg_ref[...]
        o_ref[...] = y


def _peer_dense(h_bf, tabs, u_bf, v_bf, layer, x, gate, final_g, td, et, rows_per_mod, final_norm):
    t, d = x.shape
    ne = u_bf.shape[1] // et
    r = gate.shape[1]
    nk = PEER_NKEYS
    tiles_per_mod = rows_per_mod // td
    row_spec = pl.BlockSpec((PEER_HEADS, nk, td), lambda i, e: (0, 0, i))
    pk_spec = pl.BlockSpec((PEER_HEADS, nk // 2, td), lambda i, e: (0, 0, i))
    half = et // 2
    nh = 2 * ne
    kern = functools.partial(_peer_dense_kernel, et=et, td=td, final_norm=final_norm)
    return pl.pallas_call(
        kern,
        grid=(t // td, ne + 1),
        in_specs=[
            pl.BlockSpec((td, d), lambda i, e: (i, 0)),
            pk_spec, row_spec, row_spec, pk_spec,
            pl.BlockSpec((None, et, d), lambda i, e: (layer, jnp.minimum(e, ne - 1), 0)),
            pl.BlockSpec((None, et, d), lambda i, e: (layer, jnp.maximum(e - 1, 0), 0)),
            pl.BlockSpec((td, d), lambda i, e: (i, 0)),
            pl.BlockSpec((None, r, d), lambda i, e: (i // tiles_per_mod, 0, 0)),
            pl.BlockSpec((1, d), lambda i, e: (0, 0)),
        ],
        out_specs=pl.BlockSpec((td, d), lambda i, e: (i, 0)),
        out_shape=jax.ShapeDtypeStruct((t, d), F32),
        scratch_shapes=[pltpu.VMEM((td, d), F32), pltpu.VMEM((half, td), F32), pltpu.VMEM((half, td), F32),
                        pltpu.VMEM((half // 2, td), jnp.uint32), pltpu.VMEM((half // 2, td), jnp.uint32)],
        compiler_params=_cparams(("arbitrary", "arbitrary")),
        name="peer_dense",
    )(h_bf, *tabs, u_bf, v_bf, x, gate, final_g)


def _peer_layer(x, g, shift, scale, gate, wq_t_bf, keys, u_bf, v_bf, layer, final_g, cand, *,
                tp, td, et, rows_per_mod, final_norm):
    h_bf, *tabs = _peer_select(x, g, shift, scale, wq_t_bf, keys, cand[0], cand[1], tp, rows_per_mod)
    return _peer_dense(h_bf, tabs, u_bf, v_bf, layer, x, gate, final_g, td, et, rows_per_mod, final_norm)


PEER_SAMPLE_ROWS = LANES
SAMPLE_ROWS = 16
PAST_LEN = 16384


def kernel(x_prompt, x_sample, state_pool, cache_dil_kv0, cache_dil_kv1, cache_dil_kv2, state_ret,
           c_prompt, c_sample, norm1_g, norm2_g, mod_w, mod_b, pool_w, pool_scale,
           dil_w_in, dil_w_out, ret_w_in, ret_gn_g, ret_w_out,
           peer_w_q, peer_keys, peer_u, peer_v, final_g):
    bp, seq, d = x_prompt.shape
    bs = x_sample.shape[0]
    depth = mod_w.shape[0]
    tp_rows = bp * seq
    caches = (cache_dil_kv0, cache_dil_kv1, cache_dil_kv2)
    ret_dk = d // RET_HEADS
    ret_dv = 2 * ret_dk

    sr = SAMPLE_ROWS
    n_c = bp + sr
    c_rows = -(-n_c // SUBLANES) * SUBLANES
    c_all = jnp.concatenate([c_prompt, c_sample, jnp.zeros((c_rows - bp - bs, d), F32)], axis=0)
    mods = _adaln(c_all, mod_w, mod_b).reshape(depth, c_rows, 6, d)
    pad_rows = lambda a, rows: jnp.pad(a, ((0, rows - a.shape[0]), (0, 0)))

    slope_tab = jnp.asarray(_alibi_table())
    ret_tabs = _ret_tables(ret_dk, ret_dv)
    cand = _cand_tables()
    final_g2 = final_g.reshape(1, d)
    u_bf = peer_u.astype(BF16)
    v_bf = peer_v.astype(BF16)
    wq_t_all = jnp.swapaxes(peer_w_q, 1, 2).astype(BF16)

    xp = x_prompt.reshape(tp_rows, d)
    xs = pad_rows(x_sample.reshape(bs, d), sr)
    pool_p, pool_s, ret_p, ret_s = [], [], [], []
    kv_p = [[] for _ in DIL_PATTERNS]
    kv_s = [[] for _ in DIL_PATTERNS]

    for i in range(depth):
        kind, j = i % 3, i // 3
        mp = [mods[i, :bp, m].reshape(bp, 1, d) for m in range(6)]
        ms = [mods[i, bp:bp + sr, m] for m in range(6)]
        g1 = norm1_g[i].reshape(1, d)
        g2 = norm2_g[i].reshape(1, d)

        if kind == 0:
            w_bf = pool_w[j].astype(BF16)
            ps = pool_scale[j].reshape(1, d)
            xp3, st_p = _pool_layer(xp.reshape(bp, seq, d), g1, mp[0], mp[1], mp[2],
                                    jnp.zeros((bp, CARRY, d), F32), w_bf, ps,
                                    tr=512, n_real=512, pos0=0)
            xp = xp3.reshape(tp_rows, d)
            pool_p.append(st_p[:, 1:])
            xs_pad = jnp.pad(xs[:bs].reshape(bs, 1, d), ((0, 0), (0, SUBLANES - 1), (0, 0)))
            hist = jnp.pad(state_pool[j], ((0, 0), (1, 0), (0, 0)))
            xs3, st_s = _pool_layer(xs_pad, g1, ms[0][:bs].reshape(bs, 1, d), ms[1][:bs].reshape(bs, 1, d),
                                    ms[2][:bs].reshape(bs, 1, d), hist, w_bf, ps,
                                    tr=SUBLANES, n_real=1, pos0=PAST_LEN)
            xs = pad_rows(xs3[:, 0], sr)
            pool_s.append(st_s[:, 1:])
        elif kind == 1:
            w_in = dil_w_in[j].astype(BF16)
            w_out = dil_w_out[j].astype(BF16)
            hh, hd = DIL_HEADS, DIL_HEAD_DIM
            ng = len(DIL_PATTERNS)
            spg = hh * hd // SLOT

            def head_major(sl, rows):
                return sl.reshape(spg, rows, SLOT // hd, hd).transpose(1, 0, 2, 3).reshape(rows, hh, hd)

            slots = _proj_in(xp, g1, mp[0], mp[1], w_in, 1024, seq)
            o = _dil_prompt_attn(slots, slope_tab, bp, seq)
            xp = _proj_out(o, w_out, xp, mp[2], 1024, seq)
            for g, (wdw, dil) in enumerate(DIL_PATTERNS):
                nl = min(wdw, seq)
                kk = head_major(slots[(ng + g) * spg:(ng + g + 1) * spg], tp_rows).reshape(bp, seq, hh, hd)
                vv = head_major(slots[(2 * ng + g) * spg:(2 * ng + g + 1) * spg], tp_rows).reshape(bp, seq, hh, hd)
                kv_p[g].append(jnp.stack([kk[:, seq - nl:], vv[:, seq - nl:]], axis=2))

            slots_s = _proj_in(xs, g1, ms[0][None], ms[1][None], w_in, sr, sr)
            o_s = _dil_sample_attn(slots_s, [c[j] for c in caches], slope_tab)
            xs = _proj_out(pad_rows(o_s.reshape(bs, hh * hd), sr), w_out, xs, ms[2][None], sr, sr)
            for g in range(ng):
                kn = head_major(slots_s[(ng + g) * spg:(ng + g + 1) * spg], sr)[:bs]
                vn = head_major(slots_s[(2 * ng + g) * spg:(2 * ng + g + 1) * spg], sr)[:bs]
                new = jnp.stack([kn, vn], axis=1)[:, None]
                kv_s[g].append(jnp.concatenate([caches[g][j][:, 1:], new], axis=1))
        else:
            w_in = ret_w_in[j].astype(BF16)
            w_out = ret_w_out[j].astype(BF16)
            gn = ret_gn_g[j].reshape(1, RET_HEADS * ret_dv)
            slots = _proj_in(xp, g1, mp[0], mp[1], w_in, 1024, seq)
            y, st_p = _ret_prompt(slots, ret_tabs, gn, bp, seq, ret_dk, ret_dv)
            xp = _proj_out(y, w_out, xp, mp[2], 1024, seq)
            ret_p.append(st_p)
            slots_s = _proj_in(xs, g1, ms[0][None], ms[1][None], w_in, sr, sr)
            y_s, st_s = _ret_sample(slots_s, state_ret[j], ret_tabs, gn, ret_dk, ret_dv)
            xs = _proj_out(pad_rows(y_s.reshape(bs, RET_HEADS * ret_dv), sr), w_out, xs, ms[2][None], sr, sr)
            ret_s.append(st_s)

        last = i == depth - 1
        wq_t = wq_t_all[i]
        xp = _peer_layer(xp, g2, mp[3], mp[4], mp[5], wq_t, peer_keys[i], u_bf, v_bf, i, final_g2, cand,
                         tp=256, td=512, et=512, rows_per_mod=seq, final_norm=last)
        padr = lambda a: pad_rows(a, PEER_SAMPLE_ROWS)
        xs_new = _peer_layer(padr(xs), g2, padr(ms[3])[None], padr(ms[4])[None], padr(ms[5])[None],
                             wq_t, peer_keys[i], u_bf, v_bf, i, final_g2, cand,
                             tp=PEER_SAMPLE_ROWS, td=PEER_SAMPLE_ROWS, et=512,
                             rows_per_mod=PEER_SAMPLE_ROWS, final_norm=last)
        xs = xs_new[:sr]

    y_prompt = xp.reshape(bp, seq, d)
    y_sample = xs[:bs].reshape(bs, 1, d)
    return (y_prompt, y_sample,
            jnp.stack(pool_p), jnp.stack(pool_s),
            jnp.stack(kv_p[0]), jnp.stack(kv_s[0]),
            jnp.stack(kv_p[1]), jnp.stack(kv_s[1]),
            jnp.stack(kv_p[2]), jnp.stack(kv_s[2]),
            jnp.stack(ret_p), jnp.stack(ret_s))
```

```python
import functools
import math

import jax
import jax.numpy as jnp
import numpy as np
from jax import lax
from jax.experimental import pallas as pl
from jax.experimental.pallas import tpu as pltpu

F32 = jnp.float32
BF16 = jnp.bfloat16
EPS = 1e-6
NEG_INF = float("-inf")

LANES = 128
SUBLANES = 8
BF16_ROWS = 16
VMEM_LIMIT_BYTES = 56 * 1024 * 1024

POOL_WINDOWS = (2, 4, 8, 16)
POOL_HIST = 15
DIL_PATTERNS = ((128, 1), (512, 4), (2048, 16))
DIL_HEADS = 8
DIL_HEAD_DIM = 128
DIL_BLOCK = 128
RET_HEADS = 8
RET_CHUNK = 128
PEER_HEADS = 8
PEER_NKEYS = 128
PEER_TOPK = 16
SLOT = 1024


def _cparams(sem, vmem=VMEM_LIMIT_BYTES, flags=None):
    return pltpu.CompilerParams(dimension_semantics=sem, vmem_limit_bytes=vmem, flags=flags)


def _rms_mod(x, g, shift, scale):
    ms = jnp.mean(x * x, axis=-1, keepdims=True)
    y = x * lax.rsqrt(ms + EPS) * g
    return y * (1.0 + scale) + shift


def _dot(a, b):
    return jnp.dot(a, b, preferred_element_type=F32)


def _dot_nt(a, b):
    return lax.dot_general(a, b, (((1,), (1,)), ((), ())), preferred_element_type=F32)


def _dot_tn(a, b):
    return lax.dot_general(a, b, (((0,), (0,)), ((), ())), preferred_element_type=F32)


def _adaln_kernel(c_ref, w_ref, b_ref, o_ref):
    c = c_ref[...]
    a = c * (1.0 / (1.0 + jnp.exp(-c)))
    a_hi = a.astype(BF16)
    a_lo = (a - a_hi.astype(F32)).astype(BF16)
    w = w_ref[...]
    w_hi = w.astype(BF16)
    w_lo = (w - w_hi.astype(F32)).astype(BF16)
    acc = _dot(a_hi, w_hi) + _dot(a_hi, w_lo) + _dot(a_lo, w_hi)
    o_ref[...] = acc + b_ref[...]


def _adaln(c_all, mod_w, mod_b, tn=1024):
    depth, d, n = mod_w.shape
    rows = c_all.shape[0]
    return pl.pallas_call(
        _adaln_kernel,
        grid=(depth, n // tn),
        in_specs=[
            pl.BlockSpec((rows, d), lambda l, j: (0, 0)),
            pl.BlockSpec((None, d, tn), lambda l, j: (l, 0, j)),
            pl.BlockSpec((None, 1, tn), lambda l, j: (l, 0, j)),
        ],
        out_specs=pl.BlockSpec((None, rows, tn), lambda l, j: (l, 0, j)),
        out_shape=jax.ShapeDtypeStruct((depth, rows, n), F32),
        compiler_params=_cparams(("arbitrary", "arbitrary")),
        name="adaln",
    )(c_all, mod_w, mod_b.reshape(depth, 1, n))


def _proj_in_kernel(x_ref, g_ref, sh_ref, sc_ref, w_ref, o_ref, h_ref):
    @pl.when(pl.program_id(1) == 0)
    def _():
        h_ref[...] = _rms_mod(x_ref[...], g_ref[...], sh_ref[...], sc_ref[...]).astype(BF16)

    o_ref[...] = _dot(h_ref[...], w_ref[...])


def _proj_in(x, g, shift, scale, w_bf, tm, rows_per_mod):
    t, d = x.shape
    n = w_bf.shape[1]
    r = shift.shape[1]
    tiles_per_mod = rows_per_mod // tm
    mod_spec = pl.BlockSpec((None, r, d), lambda i, j: (i // tiles_per_mod, 0, 0))
    return pl.pallas_call(
        _proj_in_kernel,
        grid=(t // tm, n // SLOT),
        in_specs=[
            pl.BlockSpec((tm, d), lambda i, j: (i, 0)),
            pl.BlockSpec((1, d), lambda i, j: (0, 0)),
            mod_spec,
            mod_spec,
            pl.BlockSpec((d, SLOT), lambda i, j: (0, j)),
        ],
        out_specs=pl.BlockSpec((None, tm, SLOT), lambda i, j: (j, i, 0)),
        out_shape=jax.ShapeDtypeStruct((n // SLOT, t, SLOT), F32),
        scratch_shapes=[pltpu.VMEM((tm, d), BF16)],
        compiler_params=_cparams(("arbitrary", "arbitrary")),
        name="proj_in",
    )(x, g, shift, scale, w_bf)


def _proj_out_kernel(a_ref, w_ref, x_ref, gate_ref, o_ref):
    y = _dot(a_ref[...].astype(BF16), w_ref[...])
    o_ref[...] = x_ref[...] + gate_ref[...] * y


def _proj_out(a, w_bf, x, gate, tm, rows_per_mod, tn=512):
    t, k = a.shape
    d = w_bf.shape[1]
    r = gate.shape[1]
    tiles_per_mod = rows_per_mod // tm
    return pl.pallas_call(
        _proj_out_kernel,
        grid=(t // tm, d // tn),
        in_specs=[
            pl.BlockSpec((tm, k), lambda i, j: (i, 0)),
            pl.BlockSpec((k, tn), lambda i, j: (0, j)),
            pl.BlockSpec((tm, tn), lambda i, j: (i, j)),
            pl.BlockSpec((None, r, tn), lambda i, j: (i // tiles_per_mod, 0, j)),
        ],
        out_specs=pl.BlockSpec((tm, tn), lambda i, j: (i, j)),
        out_shape=jax.ShapeDtypeStruct((t, d), F32),
        compiler_params=_cparams(("arbitrary", "arbitrary")),
        name="proj_out",
    )(a, w_bf, x, gate)


CARRY = 16


def _pool_kernel(x_ref, g_ref, sh_ref, sc_ref, gate_ref, hist_ref, w_ref, ps_ref,
                 o_ref, st_ref, ext_ref, *, tr, n_real, pos0):
    t = pl.program_id(1)
    d = x_ref.shape[-1]
    grp = d // len(POOL_WINDOWS)

    @pl.when(t == 0)
    def _():
        ext_ref[0:CARRY, :] = hist_ref[...]

    x = x_ref[...]
    h = _rms_mod(x, g_ref[...], sh_ref[...], sc_ref[...])
    ext_ref[CARRY:CARRY + tr, :] = h

    pos = (pos0 + t * tr + lax.broadcasted_iota(jnp.int32, (tr, 1), 0)).astype(F32)
    ys = []
    for gi, wdw in enumerate(POOL_WINDOWS):
        c0 = gi * grp
        cur = ext_ref[CARRY:CARRY + tr, c0:c0 + grp]
        acc = cur
        for k in range(1, wdw):
            acc = acc + ext_ref[CARRY - k:CARRY - k + tr, c0:c0 + grp]
        cnt = jnp.minimum(float(wdw), pos + 1.0)
        dd = acc / cnt - cur
        ys.append(_dot(dd.astype(BF16), w_ref[gi]))
    y = jnp.concatenate(ys, axis=-1) * ps_ref[...]
    o_ref[...] = x + gate_ref[...] * y

    st_ref[...] = ext_ref[n_real:n_real + CARRY, :]
    ext_ref[0:CARRY, :] = ext_ref[tr:tr + CARRY, :]


def _pool_layer(x, g, shift, scale, gate, hist16, w_bf, pscale, *, tr, n_real, pos0):
    b, s, d = x.shape
    grp = d // len(POOL_WINDOWS)
    mod_spec = pl.BlockSpec((None, 1, d), lambda i, t: (i, 0, 0))
    kern = functools.partial(_pool_kernel, tr=tr, n_real=n_real, pos0=pos0)
    return pl.pallas_call(
        kern,
        grid=(b, s // tr),
        in_specs=[
            pl.BlockSpec((None, tr, d), lambda i, t: (i, t, 0)),
            pl.BlockSpec((1, d), lambda i, t: (0, 0)),
            mod_spec, mod_spec, mod_spec,
            pl.BlockSpec((None, CARRY, d), lambda i, t: (i, 0, 0)),
            pl.BlockSpec((len(POOL_WINDOWS), grp, grp), lambda i, t: (0, 0, 0)),
            pl.BlockSpec((1, d), lambda i, t: (0, 0)),
        ],
        out_specs=[
            pl.BlockSpec((None, tr, d), lambda i, t: (i, t, 0)),
            pl.BlockSpec((None, CARRY, d), lambda i, t: (i, 0, 0)),
        ],
        out_shape=[
            jax.ShapeDtypeStruct((b, s, d), F32),
            jax.ShapeDtypeStruct((b, CARRY, d), F32),
        ],
        scratch_shapes=[pltpu.VMEM((CARRY + tr, d), F32)],
        compiler_params=_cparams(("arbitrary", "arbitrary")),
        name="pool",
    )(x, g, shift, scale, gate, hist16, w_bf, pscale)


def _alibi_table():
    n = len(DIL_PATTERNS) * DIL_HEADS
    slopes = np.exp2(-8.0 * np.arange(1, n + 1, dtype=np.float64) / n).reshape(len(DIL_PATTERNS), DIL_HEADS)
    dil = np.array([p[1] for p in DIL_PATTERNS], np.float64)[:, None]
    tab = (slopes * dil).T
    return np.broadcast_to(tab[:, :, None], (DIL_HEADS, len(DIL_PATTERNS), LANES)).astype(np.float32)


def _dil_prompt_kernel(*refs, seq):
    q_refs = refs[0:3]
    k_refs = refs[3:6]
    v_refs = refs[6:9]
    slope_ref = refs[9]
    o_ref = refs[10]
    qd, kd, vd, og, lg = refs[11:16]
    oi = refs[16:19]
    li = refs[19:22]
    blk = DIL_BLOCK
    hd = DIL_HEAD_DIM
    scale = hd ** -0.5

    row = lax.broadcasted_iota(jnp.int32, (blk, blk), 0)
    col = lax.broadcasted_iota(jnp.int32, (blk, blk), 1)
    dist_c = (row - col).astype(F32)
    mask_c = row >= col
    dist_p = dist_c + float(blk)
    mask_p = col >= row

    for g, (wdw, dil) in enumerate(DIL_PATTERNS):
        ls = seq // dil
        nb = ls // blk
        if dil == 1:
            qs, ks, vs = q_refs[g], k_refs[g], v_refs[g]
        else:
            for res in range(dil):
                qd[res * ls:(res + 1) * ls, :] = q_refs[g][pl.ds(res, ls, stride=dil), :]
                kd[res * ls:(res + 1) * ls, :] = k_refs[g][pl.ds(res, ls, stride=dil), :]
                vd[res * ls:(res + 1) * ls, :] = v_refs[g][pl.ds(res, ls, stride=dil), :]
            qs, ks, vs = qd, kd, vd
        slope = slope_ref[g:g + 1, :]
        bias_c = slope * dist_c
        bias_p = slope * dist_p

        def step(it, carry, qs=qs, ks=ks, vs=vs, nb=nb, bias_c=bias_c, bias_p=bias_p):
            r0 = pl.multiple_of(it * blk, blk)
            has_prev = (it % nb) != 0
            rp = pl.multiple_of(jnp.maximum(it - 1, 0) * blk, blk)
            qb = qs[pl.ds(r0, blk), :].astype(BF16)
            kc = ks[pl.ds(r0, blk), :].astype(BF16)
            kp = ks[pl.ds(rp, blk), :].astype(BF16)
            s_c = _dot_nt(qb, kc) * scale - bias_c
            s_p = _dot_nt(qb, kp) * scale - bias_p
            pen = jnp.where(has_prev, 0.0, NEG_INF)
            s_c = jnp.where(mask_c, s_c, NEG_INF)
            s_p = jnp.where(mask_p, s_p + pen, NEG_INF)
            m = jnp.maximum(jnp.max(s_c, axis=-1, keepdims=True), jnp.max(s_p, axis=-1, keepdims=True))
            p_c = jnp.exp(s_c - m)
            p_p = jnp.exp(s_p - m)
            den = jnp.sum(p_c, axis=-1, keepdims=True) + jnp.sum(p_p, axis=-1, keepdims=True)
            vc = vs[pl.ds(r0, blk), :].astype(BF16)
            vp = vs[pl.ds(rp, blk), :].astype(BF16)
            o = (_dot(p_c.astype(BF16), vc) + _dot(p_p.astype(BF16), vp)) / den
            og[pl.ds(r0, blk), :] = o
            lg[pl.ds(r0, blk), :] = jnp.broadcast_to(m + jnp.log(den), (blk, hd))
            return carry

        lax.fori_loop(0, seq // blk, step, 0, unroll=4)

        if dil == 1:
            oi[g][...] = og[...]
            li[g][...] = lg[...]
        else:
            for res in range(dil):
                oi[g][pl.ds(res, ls, stride=dil), :] = og[res * ls:(res + 1) * ls, :]
                li[g][pl.ds(res, ls, stride=dil), :] = lg[res * ls:(res + 1) * ls, :]

    l0, l1, l2 = li[0][...], li[1][...], li[2][...]
    mx = jnp.maximum(jnp.maximum(l0, l1), l2)
    e0, e1, e2 = jnp.exp(l0 - mx), jnp.exp(l1 - mx), jnp.exp(l2 - mx)
    o_ref[...] = (e0 * oi[0][...] + e1 * oi[1][...] + e2 * oi[2][...]) / (e0 + e1 + e2)


def _dil_prompt_attn(slots, slope_tab, batch, seq):
    hd = DIL_HEAD_DIM
    per_slot = SLOT // hd
    ng = len(DIL_PATTERNS)
    slots_per_group = (DIL_HEADS * hd) // SLOT

    def spec(kind, g):
        base = (kind * ng + g) * slots_per_group
        return pl.BlockSpec((None, seq, hd), lambda n, h: (base + h // per_slot, n, h % per_slot))

    in_specs = [spec(kind, g) for kind in range(3) for g in range(ng)]
    in_specs.append(pl.BlockSpec((None, ng, LANES), lambda n, h: (h, 0, 0)))
    kern = functools.partial(_dil_prompt_kernel, seq=seq)
    return pl.pallas_call(
        kern,
        grid=(batch, DIL_HEADS),
        in_specs=in_specs,
        out_specs=pl.BlockSpec((seq, hd), lambda n, h: (n, h)),
        out_shape=jax.ShapeDtypeStruct((batch * seq, DIL_HEADS * hd), F32),
        scratch_shapes=[pltpu.VMEM((seq, hd), F32) for _ in range(5 + 2 * ng)],
        compiler_params=_cparams(("arbitrary", "arbitrary")),
        name="dil_attn_prompt",
    )(*([slots] * 9), slope_tab)


def _dil_sample_kernel(slots_ref, c0_ref, c1_ref, c2_ref, slope_ref, o_ref):
    n = pl.program_id(0)
    hd = DIL_HEAD_DIM
    hh = DIL_HEADS
    ng = len(DIL_PATTERNS)
    assert SLOT == hh * hd
    scale = hd ** -0.5
    caches = (c0_ref, c1_ref, c2_ref)
    nk = DIL_BLOCK
    steps = (nk - lax.broadcasted_iota(jnp.int32, (nk, 1, 1), 0)).astype(F32)

    def heads(kind, g):
        full = slots_ref[kind * ng + g, pl.ds(n, 1), :]
        return _stack_rows([full[:, h * hd:(h + 1) * hd] for h in range(hh)])

    o_g, l_g = [], []
    for g in range(ng):
        q, kn, vn = heads(0, g), heads(1, g), heads(2, g)
        kc = caches[g][:, 0]
        vc = caches[g][:, 1]
        slope = slope_ref[:, g, 0:1]
        s_c = jnp.sum(kc * q, axis=-1, keepdims=True) * scale - slope * steps
        s_n = jnp.sum(kn * q, axis=-1, keepdims=True) * scale
        m = jnp.maximum(jnp.max(s_c, axis=0), s_n)
        p_c = jnp.exp(s_c - m)
        p_n = jnp.exp(s_n - m)
        den = jnp.sum(p_c, axis=0) + p_n
        o_g.append((jnp.sum(p_c * vc, axis=0) + p_n * vn) / den)
        l_g.append(m + jnp.log(den))
    mx = jnp.maximum(jnp.maximum(l_g[0], l_g[1]), l_g[2])
    e = [jnp.exp(l - mx) for l in l_g]
    o = (e[0] * o_g[0] + e[1] * o_g[1] + e[2] * o_g[2]) / (e[0] + e[1] + e[2])
    o_ref[...] = jnp.concatenate([o[h:h + 1] for h in range(hh)], axis=-1)


def _dil_sample_attn(slots, caches, slope_tab):
    b = caches[0].shape[0]
    ins = []
    specs = [pl.BlockSpec(slots.shape, lambda n: (0, 0, 0))]
    for g, (wdw, dil) in enumerate(DIL_PATTERNS):
        ins.append(caches[g].reshape(b, wdw // dil, dil, 2, DIL_HEADS, DIL_HEAD_DIM))
        specs.append(pl.BlockSpec((None, wdw // dil, None, 2, DIL_HEADS, DIL_HEAD_DIM),
                                  lambda n: (n, 0, 0, 0, 0, 0)))
    specs.append(pl.BlockSpec(slope_tab.shape, lambda n: (0, 0, 0)))
    return pl.pallas_call(
        _dil_sample_kernel,
        grid=(b,),
        in_specs=specs,
        out_specs=pl.BlockSpec((None, 1, DIL_HEADS * DIL_HEAD_DIM), lambda n: (n, 0, 0)),
        out_shape=jax.ShapeDtypeStruct((b, 1, DIL_HEADS * DIL_HEAD_DIM), F32),
        compiler_params=_cparams(("arbitrary",)),
        name="dil_attn_sample",
    )(slots, *ins, slope_tab)


def _ret_tables(dk, dv):
    c = RET_CHUNK
    log_g = np.log1p(-np.exp2(-5.0 - np.arange(RET_HEADS, dtype=np.float64)))
    pos = np.arange(c, dtype=np.float64)
    diff = pos[:, None] - pos[None, :]
    dmask = np.where(diff >= 0, np.exp(log_g[:, None, None] * np.maximum(diff, 0.0)), 0.0)
    q_dec = np.exp(log_g[:, None] * (pos[None, :] + 1.0))
    k_dec = np.exp(log_g[:, None] * (c - 1.0 - pos[None, :]))
    c_dec = np.exp(log_g * c)
    gam = np.exp(log_g)
    f = lambda a: jnp.asarray(a.astype(np.float32))
    return dict(
        dmask=f(dmask),
        q_dec=f(np.broadcast_to(q_dec[:, :, None], (RET_HEADS, c, dk))),
        k_dec=f(np.broadcast_to(k_dec[:, :, None], (RET_HEADS, c, dk))),
        c_dec=f(np.broadcast_to(c_dec[:, None, None], (RET_HEADS, 1, LANES))),
        gamma=f(np.broadcast_to(gam[:, None, None], (RET_HEADS, 1, LANES))),
    )


def _gn_gate(o, gn, gate):
    mu = jnp.mean(o, axis=-1, keepdims=True)
    cen = o - mu
    var = jnp.mean(cen * cen, axis=-1, keepdims=True)
    y = cen * lax.rsqrt(var + EPS) * gn
    return gate * (1.0 / (1.0 + jnp.exp(-gate))) * y


def _ret_prompt_kernel(q_ref, k_ref, v_ref, gt_ref, dm_ref, qd_ref, kd_ref, cd_ref, gn_ref,
                       y_ref, st_ref, state_ref, *, seq):
    c = RET_CHUNK
    dk = q_ref.shape[-1]
    state_ref[...] = jnp.zeros_like(state_ref)
    dmask = dm_ref[...]
    q_dec = qd_ref[...]
    k_dec = kd_ref[...]
    c_dec = cd_ref[:, 0:1]
    gn = gn_ref[...]
    kscale = dk ** -0.5

    def step(ci, carry):
        r0 = pl.multiple_of(ci * c, c)
        qc = q_ref[pl.ds(r0, c), :]
        kc = k_ref[pl.ds(r0, c), :] * kscale
        vc = v_ref[pl.ds(r0, c), :].astype(BF16)
        state = state_ref[...]
        att = _dot_nt(qc.astype(BF16), kc.astype(BF16)) * dmask
        o = _dot(att.astype(BF16), vc) + _dot((qc * q_dec).astype(BF16), state.astype(BF16))
        state_ref[...] = c_dec * state + _dot_tn((kc * k_dec).astype(BF16), vc)
        y_ref[pl.ds(r0, c), :] = _gn_gate(o, gn, gt_ref[pl.ds(r0, c), :]).astype(y_ref.dtype)
        return carry

    lax.fori_loop(0, seq // c, step, 0)
    st_ref[...] = state_ref[...]


def _ret_prompt(slots, tabs, gn, batch, seq, dk, dv):
    hq = SLOT // dk
    nq = RET_HEADS // hq
    hv = SLOT // dv
    nv = RET_HEADS // hv
    kern = functools.partial(_ret_prompt_kernel, seq=seq)
    return pl.pallas_call(
        kern,
        grid=(batch, RET_HEADS),
        in_specs=[
            pl.BlockSpec((None, seq, dk), lambda n, h: (h // hq, n, h % hq)),
            pl.BlockSpec((None, seq, dk), lambda n, h: (nq + h // hq, n, h % hq)),
            pl.BlockSpec((None, seq, dv), lambda n, h: (2 * nq + h // hv, n, h % hv)),
            pl.BlockSpec((None, seq, dv), lambda n, h: (2 * nq + nv + h // hv, n, h % hv)),
            pl.BlockSpec((None, RET_CHUNK, RET_CHUNK), lambda n, h: (h, 0, 0)),
            pl.BlockSpec((None, RET_CHUNK, dk), lambda n, h: (h, 0, 0)),
            pl.BlockSpec((None, RET_CHUNK, dk), lambda n, h: (h, 0, 0)),
            pl.BlockSpec((None, 1, LANES), lambda n, h: (h, 0, 0)),
            pl.BlockSpec((1, dv), lambda n, h: (0, h)),
        ],
        out_specs=[
            pl.BlockSpec((seq, dv), lambda n, h: (n, h)),
            pl.BlockSpec((None, None, dk, dv), lambda n, h: (n, h, 0, 0)),
        ],
        out_shape=[
            jax.ShapeDtypeStruct((batch * seq, RET_HEADS * dv), BF16),
            jax.ShapeDtypeStruct((batch, RET_HEADS, dk, dv), F32),
        ],
        scratch_shapes=[pltpu.VMEM((dk, dv), F32)],
        compiler_params=_cparams(("arbitrary", "arbitrary")),
        name="ret_prompt",
    )(slots, slots, slots, slots, tabs["dmask"], tabs["q_dec"], tabs["k_dec"], tabs["c_dec"], gn)


def _ret_sample_kernel(slots_ref, s0_ref, gam_ref, gn_ref, y_ref, st_ref, *, dk, dv):
    n = pl.program_id(0)
    h = pl.program_id(1)
    hq = SLOT // dk
    nq = RET_HEADS // hq
    hv = SLOT // dv
    nv = RET_HEADS // hv
    gam = gam_ref[:, 0:1]

    def head_row(base, per_slot, width):
        full = slots_ref[base + h // per_slot, pl.ds(n, 1), :]
        parts = [full[:, i * width:(i + 1) * width] for i in range(per_slot)]
        out = parts[0]
        for i in range(1, per_slot):
            out = jnp.where(h % per_slot == i, parts[i], out)
        return out

    q = head_row(0, hq, dk)
    k = head_row(nq, hq, dk) * (dk ** -0.5)
    v = head_row(2 * nq, hv, dv)
    gate = head_row(2 * nq + nv, hv, dv)

    eye = lax.broadcasted_iota(jnp.int32, (dk, dk), 0) == lax.broadcasted_iota(jnp.int32, (dk, dk), 1)
    q_col = jnp.sum(jnp.where(eye, q, 0.0), axis=-1, keepdims=True)
    k_col = jnp.sum(jnp.where(eye, k, 0.0), axis=-1, keepdims=True)
    state = s0_ref[...]
    qk = jnp.sum(q * k, axis=-1, keepdims=True)
    o = qk * v + gam * jnp.sum(q_col * state, axis=0, keepdims=True)
    st_ref[...] = gam * state + k_col * v
    y_ref[...] = _gn_gate(o, gn_ref[...], gate)


def _ret_sample(slots, s0, tabs, gn, dk, dv):
    b = s0.shape[0]
    kern = functools.partial(_ret_sample_kernel, dk=dk, dv=dv)
    return pl.pallas_call(
        kern,
        grid=(b, RET_HEADS),
        in_specs=[
            pl.BlockSpec(slots.shape, lambda n, h: (0, 0, 0)),
            pl.BlockSpec((None, None, dk, dv), lambda n, h: (n, h, 0, 0)),
            pl.BlockSpec((None, 1, LANES), lambda n, h: (h, 0, 0)),
            pl.BlockSpec((1, dv), lambda n, h: (0, h)),
        ],
        out_specs=[
            pl.BlockSpec((None, 1, dv), lambda n, h: (n, 0, h)),
            pl.BlockSpec((None, None, dk, dv), lambda n, h: (n, h, 0, 0)),
        ],
        out_shape=[
            jax.ShapeDtypeStruct((b, 1, RET_HEADS * dv), F32),
            jax.ShapeDtypeStruct((b, RET_HEADS, dk, dv), F32),
        ],
        compiler_params=_cparams(("arbitrary", "arbitrary")),
        name="ret_sample",
    )(slots, s0, tabs["gamma"], gn)


def _cand_tables():
    k = PEER_TOPK
    rows, cols = [], []
    rows += [0] * k
    cols += list(range(k))
    for r in range(1, SUBLANES):
        rows += [r] * SUBLANES
        cols += list(range(SUBLANES))
    rows += list(range(SUBLANES, k))
    cols += [0] * (k - SUBLANES)
    rows, cols = np.array(rows), np.array(cols)
    valid = (rows + 1) * (cols + 1) <= k
    pos = (rows * k + cols).astype(np.float32)
    bias = np.where(valid, 0.0, CAND_PAD).astype(np.float32)
    f = lambda a: jnp.asarray(np.broadcast_to(a[:, None], (a.shape[0], LANES)).copy())
    return f(pos), f(bias)


MARK = float(2 ** 100)
CAND_PAD = -float(2 ** 90)


def _marker(r):
    return -MARK * (1.0 + r / 64.0)


def _pack_bf16(x):
    return pltpu.bitcast(x.astype(BF16), jnp.uint32)


def _unpack_bf16(x):
    return pltpu.bitcast(x, BF16)


def _stack_rows(rows):
    n = len(rows)
    idx = lax.broadcasted_iota(jnp.int32, (n, rows[0].shape[-1]), 0)
    out = jnp.broadcast_to(rows[n - 1], idx.shape)
    for i in range(n - 2, -1, -1):
        out = jnp.where(idx == i, rows[i], out)
    return out


def _extract(z, k, pos):
    vals = []
    for r in range(k):
        m = jnp.max(z, axis=0, keepdims=True)
        hit = z == m
        if pos is not None:
            first = jnp.min(jnp.where(hit, pos, MARK), axis=0, keepdims=True)
            hit = pos == first
        z = jnp.where(hit, _marker(r), z)
        vals.append(m)
    return z, vals


def _count_marked(z):
    return jnp.sum(jnp.where(z <= -MARK, 1.0, 0.0), axis=0, keepdims=True)


def _select_block(s0, s1, key_pos, cpos, cbias, exact_ties):
    k = PEER_TOPK
    z0, a = _extract(s0, k, key_pos if exact_ties else None)
    z1, b = _extract(s1, k, key_pos if exact_ties else None)
    a_hi = _stack_rows(a[SUBLANES:])
    b_all = _stack_rows(b)
    b_lo = b_all[0:SUBLANES]
    cand = [a[0] + b_all] + [a[r] + b_lo for r in range(1, SUBLANES)] + [a_hi + b[0]]
    cand = jnp.concatenate(cand, axis=0) + cbias
    zc, _ = _extract(cand, k, cpos if exact_ties else None)
    sel = jnp.where(zc <= -MARK, 1.0, 0.0)
    top = a[0] + b[0]
    zsum = jnp.sum(sel * jnp.exp(cand - top), axis=0, keepdims=True)
    n_r = [jnp.sum(sel[0:k], axis=0, keepdims=True)]
    rows_kept = sel[0:1]
    for r in range(1, SUBLANES):
        lo = k + (r - 1) * SUBLANES
        n_r.append(jnp.sum(sel[lo:lo + SUBLANES], axis=0, keepdims=True))
        rows_kept = rows_kept + sel[lo:lo + 1]
    lo = k + (SUBLANES - 1) * SUBLANES
    rows_kept = rows_kept + jnp.sum(sel[lo:lo + SUBLANES], axis=0, keepdims=True)
    nsel = jnp.zeros(s0.shape, F32)
    for r in range(SUBLANES):
        nsel = jnp.where(z0 == _marker(r), n_r[r], nsel)
    last_kept = -MARK * (1.0 + rows_kept * (1.0 / 64.0))
    nsel = jnp.where(z0 <= _marker(SUBLANES), jnp.where(z0 > last_kept, 1.0, 0.0), nsel)
    rank1 = jnp.where(z1 <= -MARK, (z1 * (-1.0 / MARK) - 1.0) * 64.0, float(k))
    ea = jnp.exp(s0 - a[0]) / zsum
    eb = jnp.exp(s1 - b[0])
    ties = (jnp.abs(_count_marked(z0) - k) + jnp.abs(_count_marked(z1) - k)
            + jnp.abs(jnp.sum(sel, axis=0, keepdims=True) - k))
    return rank1, nsel, ea, eb, ties


def _peer_select_kernel(x_ref, g_ref, sh_ref, sc_ref, wq_ref, keys_ref, cpos_ref, cbias_ref,
                        h_ref, rank1_ref, nsel_ref, ea_ref, eb_ref, s_ref, *, tp):
    nk = PEER_NKEYS
    h = _rms_mod(x_ref[...], g_ref[...], sh_ref[...], sc_ref[...])
    hb = h.astype(BF16)
    h_ref[...] = pltpu.bitcast(hb, jnp.uint32)
    q_t = _dot_nt(wq_ref[...], hb)
    for hp in range(2 * PEER_HEADS):
        key = keys_ref[hp % 2].astype(BF16)
        s_ref[hp] = _dot(key, q_t[hp * nk:(hp + 1) * nk, :].astype(BF16))

    key_pos = lax.broadcasted_iota(jnp.int32, (nk, LANES), 0).astype(F32)

    per_it = 2 if tp % (2 * LANES) == 0 else 1
    its_per_head = tp // (per_it * LANES)

    def body(it, carry):
        head = it // its_per_head
        base = (it % its_per_head) * per_it

        def run(exact_ties):
            ties = None
            for sub in range(per_it):
                l0 = pl.multiple_of((base + sub) * LANES, LANES)
                s0 = s_ref[2 * head, :, pl.ds(l0, LANES)]
                s1 = s_ref[2 * head + 1, :, pl.ds(l0, LANES)]
                rank1, nsel, ea, eb, t = _select_block(s0, s1, key_pos, cpos_ref[...], cbias_ref[...], exact_ties)
                rank1_ref[head, :, pl.ds(l0, LANES)] = _pack_bf16(rank1)
                nsel_ref[head, :, pl.ds(l0, LANES)] = nsel
                ea_ref[head, :, pl.ds(l0, LANES)] = ea
                eb_ref[head, :, pl.ds(l0, LANES)] = _pack_bf16(eb)
                ties = t if ties is None else ties + t
            return ties

        ties = run(False)

        @pl.when(jnp.max(ties) > 0.0)
        def _():
            run(True)

        return carry

    lax.fori_loop(0, PEER_HEADS * its_per_head, body, 0)


def _peer_select(x, g, shift, scale, wq_t_bf, keys, cpos, cbias, tp, rows_per_mod):
    t, d = x.shape
    r = shift.shape[1]
    tiles_per_mod = rows_per_mod // tp
    nk = PEER_NKEYS
    mod_spec = pl.BlockSpec((None, r, d), lambda i: (i // tiles_per_mod, 0, 0))
    row_spec = pl.BlockSpec((PEER_HEADS, nk, tp), lambda i: (0, 0, i))
    row_tab = jax.ShapeDtypeStruct((PEER_HEADS, nk, t), F32)
    pk_spec = pl.BlockSpec((PEER_HEADS, nk // 2, tp), lambda i: (0, 0, i))
    pk_tab = jax.ShapeDtypeStruct((PEER_HEADS, nk // 2, t), jnp.uint32)
    kern = functools.partial(_peer_select_kernel, tp=tp)
    return pl.pallas_call(
        kern,
        grid=(t // tp,),
        in_specs=[
            pl.BlockSpec((tp, d), lambda i: (i, 0)),
            pl.BlockSpec((1, d), lambda i: (0, 0)),
            mod_spec, mod_spec,
            pl.BlockSpec(wq_t_bf.shape, lambda i: (0, 0)),
            pl.BlockSpec(keys.shape, lambda i: (0, 0, 0)),
            pl.BlockSpec(cpos.shape, lambda i: (0, 0)),
            pl.BlockSpec(cbias.shape, lambda i: (0, 0)),
        ],
        out_specs=[pl.BlockSpec((tp // 2, d), lambda i: (i, 0)), pk_spec, row_spec, row_spec, pk_spec],
        out_shape=[jax.ShapeDtypeStruct((t // 2, d), jnp.uint32), pk_tab, row_tab, row_tab, pk_tab],
        scratch_shapes=[pltpu.VMEM((2 * PEER_HEADS, nk, tp), F32)],
        compiler_params=_cparams(("arbitrary",)),
        name="peer_select",
    )(x, g, shift, scale, wq_t_bf, keys, cpos, cbias)


def _peer_dense_kernel(h_ref, rank1_ref, nsel_ref, ea_ref, eb_ref, u_ref, v_ref, x_ref, gate_ref,
                       fg_ref, o_ref, acc_ref, a0_ref, a1_ref, p0_ref, p1_ref, *, et, td, final_norm):
    e = pl.program_id(1)
    last = pl.num_programs(1) - 1
    nk = PEER_NKEYS
    half = et // 2
    nb = half // nk
    n_half_tiles = 2 * last

    @pl.when(e == 0)
    def _():
        acc_ref[...] = jnp.zeros_like(acc_ref)
        a1_ref[...] = jnp.zeros_like(a1_ref)
        p0_ref[...] = jnp.zeros_like(p0_ref)

    def consume(a_ref, p_ref, tile, live):
        kt = nk // BF16_ROWS
        tile3 = (kt, BF16_ROWS, LANES)
        for ib in range(nb):
            i = tile * nb + ib
            n_rows = [nsel_ref[hd, pl.ds(i, 1), :] for hd in range(PEER_HEADS)]
            ea_rows = [ea_ref[hd, pl.ds(i, 1), :] * live for hd in range(PEER_HEADS)]
            for lg in range(td // LANES):
                ls = slice(lg * LANES, (lg + 1) * LANES)
                a = a_ref[ib * nk:(ib + 1) * nk, ls]
                act = (0.5 * a * (1.0 + lax.erf(a * (2.0 ** -0.5)))).astype(BF16).reshape(tile3)
                w = None
                for hd in range(PEER_HEADS):
                    n_b = jnp.broadcast_to(n_rows[hd][:, ls], (BF16_ROWS, LANES)).astype(BF16)
                    ea_b = jnp.broadcast_to(ea_rows[hd][:, ls], (BF16_ROWS, LANES)).astype(BF16)
                    r1 = _unpack_bf16(rank1_ref[hd, :, ls]).reshape(tile3)
                    eb = _unpack_bf16(eb_ref[hd, :, ls]).reshape(tile3)
                    term = jnp.where(r1 < n_b, eb, 0) * ea_b
                    w = term if w is None else w + term
                p_ref[ib * (nk // 2):(ib + 1) * (nk // 2), ls] = pltpu.bitcast((w * act).reshape(nk, LANES), jnp.uint32)

    h = _unpack_bf16(h_ref[...])
    hp = half // 2
    a0_ref[...] = _dot_nt(_unpack_bf16(u_ref[0:hp, :]), h)
    d_even = _dot_tn(_unpack_bf16(p0_ref[...]), _unpack_bf16(v_ref[0:hp, :]))
    consume(a1_ref, p1_ref, jnp.maximum(2 * e - 1, 0), (e > 0).astype(F32))
    a1_ref[...] = _dot_nt(_unpack_bf16(u_ref[hp:2 * hp, :]), h)
    consume(a0_ref, p0_ref, jnp.minimum(2 * e, n_half_tiles - 1), (e < last).astype(F32))
    d_odd = _dot_tn(_unpack_bf16(p1_ref[...]), _unpack_bf16(v_ref[hp:2 * hp, :]))
    acc_ref[...] += d_even + d_odd

    @pl.when(e == last)
    def _():
        y = x_ref[...] + gate_ref[...] * acc_ref[...]
        if final_norm:
            ms = jnp.mean(y * y, axis=-1, keepdims=True)
            y = y * lax.rsqrt(ms + EPS) * fg_ref[...]
        o_ref[...] = y


def _pack_rows_kernel(x_ref, o_ref):
    o_ref[...] = pltpu.bitcast(x_ref[...].astype(BF16), jnp.uint32)


def _pack_rows(x, rows=1024):
    l, e, d = x.shape
    return pl.pallas_call(
        _pack_rows_kernel,
        grid=(l, e // rows),
        in_specs=[pl.BlockSpec((None, rows, d), lambda i, j: (i, j, 0))],
        out_specs=pl.BlockSpec((None, rows // 2, d), lambda i, j: (i, j, 0)),
        out_shape=jax.ShapeDtypeStruct((l, e // 2, d), jnp.uint32),
        compiler_params=_cparams(("arbitrary", "arbitrary")),
        name="pack_rows",
    )(x)


def _peer_dense(h_bf, tabs, u_bf, v_bf, layer, x, gate, final_g, td, et, rows_per_mod, final_norm):
    t, d = x.shape
    ne = 2 * u_bf.shape[1] // et
    r = gate.shape[1]
    nk = PEER_NKEYS
    tiles_per_mod = rows_per_mod // td
    row_spec = pl.BlockSpec((PEER_HEADS, nk, td), lambda i, e: (0, 0, i))
    pk_spec = pl.BlockSpec((PEER_HEADS, nk // 2, td), lambda i, e: (0, 0, i))
    half = et // 2
    kern = functools.partial(_peer_dense_kernel, et=et, td=td, final_norm=final_norm)
    return pl.pallas_call(
        kern,
        grid=(t // td, ne + 1),
        in_specs=[
            pl.BlockSpec((td // 2, d), lambda i, e: (i, 0)),
            pk_spec, row_spec, row_spec, pk_spec,
            pl.BlockSpec((None, half, d), lambda i, e: (layer, jnp.minimum(e, ne - 1), 0)),
            pl.BlockSpec((None, half, d), lambda i, e: (layer, jnp.maximum(e - 1, 0), 0)),
            pl.BlockSpec((td, d), lambda i, e: (i, 0)),
            pl.BlockSpec((None, r, d), lambda i, e: (i // tiles_per_mod, 0, 0)),
            pl.BlockSpec((1, d), lambda i, e: (0, 0)),
        ],
        out_specs=pl.BlockSpec((td, d), lambda i, e: (i, 0)),
        out_shape=jax.ShapeDtypeStruct((t, d), F32),
        scratch_shapes=[pltpu.VMEM((td, d), F32), pltpu.VMEM((half, td), F32), pltpu.VMEM((half, td), F32),
                        pltpu.VMEM((half // 2, td), jnp.uint32), pltpu.VMEM((half // 2, td), jnp.uint32)],
        compiler_params=_cparams(("arbitrary", "arbitrary")),
        name="peer_dense",
    )(h_bf, *tabs, u_bf, v_bf, x, gate, final_g)


def _peer_layer(x, g, shift, scale, gate, wq_t_bf, keys, u_bf, v_bf, layer, final_g, cand, *,
                tp, td, et, rows_per_mod, final_norm):
    h_bf, *tabs = _peer_select(x, g, shift, scale, wq_t_bf, keys, cand[0], cand[1], tp, rows_per_mod)
    return _peer_dense(h_bf, tabs, u_bf, v_bf, layer, x, gate, final_g, td, et, rows_per_mod, final_norm)


PEER_SAMPLE_ROWS = LANES
SAMPLE_ROWS = 16
PAST_LEN = 16384


def kernel(x_prompt, x_sample, state_pool, cache_dil_kv0, cache_dil_kv1, cache_dil_kv2, state_ret,
           c_prompt, c_sample, norm1_g, norm2_g, mod_w, mod_b, pool_w, pool_scale,
           dil_w_in, dil_w_out, ret_w_in, ret_gn_g, ret_w_out,
           peer_w_q, peer_keys, peer_u, peer_v, final_g):
    bp, seq, d = x_prompt.shape
    bs = x_sample.shape[0]
    depth = mod_w.shape[0]
    tp_rows = bp * seq
    caches = (cache_dil_kv0, cache_dil_kv1, cache_dil_kv2)
    ret_dk = d // RET_HEADS
    ret_dv = 2 * ret_dk

    sr = SAMPLE_ROWS
    n_c = bp + sr
    c_rows = -(-n_c // SUBLANES) * SUBLANES
    c_all = jnp.concatenate([c_prompt, c_sample, jnp.zeros((c_rows - bp - bs, d), F32)], axis=0)
    mods = _adaln(c_all, mod_w, mod_b).reshape(depth, c_rows, 6, d)
    pad_rows = lambda a, rows: jnp.pad(a, ((0, rows - a.shape[0]), (0, 0)))

    slope_tab = jnp.asarray(_alibi_table())
    ret_tabs = _ret_tables(ret_dk, ret_dv)
    cand = _cand_tables()
    final_g2 = final_g.reshape(1, d)
    u_bf = _pack_rows(peer_u)
    v_bf = _pack_rows(peer_v)
    wq_t_all = jnp.swapaxes(peer_w_q, 1, 2).astype(BF16)

    xp = x_prompt.reshape(tp_rows, d)
    xs = pad_rows(x_sample.reshape(bs, d), sr)
    pool_p, pool_s, ret_p, ret_s = [], [], [], []
    kv_p = [[] for _ in DIL_PATTERNS]
    kv_s = [[] for _ in DIL_PATTERNS]

    for i in range(depth):
        kind, j = i % 3, i // 3
        mp = [mods[i, :bp, m].reshape(bp, 1, d) for m in range(6)]
        ms = [mods[i, bp:bp + sr, m] for m in range(6)]
        g1 = norm1_g[i].reshape(1, d)
        g2 = norm2_g[i].reshape(1, d)

        if kind == 0:
            w_bf = pool_w[j].astype(BF16)
            ps = pool_scale[j].reshape(1, d)
            xp3, st_p = _pool_layer(xp.reshape(bp, seq, d), g1, mp[0], mp[1], mp[2],
                                    jnp.zeros((bp, CARRY, d), F32), w_bf, ps,
                                    tr=512, n_real=512, pos0=0)
            xp = xp3.reshape(tp_rows, d)
            pool_p.append(st_p[:, 1:])
            xs_pad = jnp.pad(xs[:bs].reshape(bs, 1, d), ((0, 0), (0, SUBLANES - 1), (0, 0)))
            hist = jnp.pad(state_pool[j], ((0, 0), (1, 0), (0, 0)))
            xs3, st_s = _pool_layer(xs_pad, g1, ms[0][:bs].reshape(bs, 1, d), ms[1][:bs].reshape(bs, 1, d),
                                    ms[2][:bs].reshape(bs, 1, d), hist, w_bf, ps,
                                    tr=SUBLANES, n_real=1, pos0=PAST_LEN)
            xs = pad_rows(xs3[:, 0], sr)
            pool_s.append(st_s[:, 1:])
        elif kind == 1:
            w_in = dil_w_in[j].astype(BF16)
            w_out = dil_w_out[j].astype(BF16)
            hh, hd = DIL_HEADS, DIL_HEAD_DIM
            ng = len(DIL_PATTERNS)
            spg = hh * hd // SLOT

            def head_major(sl, rows):
                return sl.reshape(spg, rows, SLOT // hd, hd).transpose(1, 0, 2, 3).reshape(rows, hh, hd)

            slots = _proj_in(xp, g1, mp[0], mp[1], w_in, 1024, seq)
            o = _dil_prompt_attn(slots, slope_tab, bp, seq)
            xp = _proj_out(o, w_out, xp, mp[2], 1024, seq)
            for g, (wdw, dil) in enumerate(DIL_PATTERNS):
                nl = min(wdw, seq)
                kk = head_major(slots[(ng + g) * spg:(ng + g + 1) * spg], tp_rows).reshape(bp, seq, hh, hd)
                vv = head_major(slots[(2 * ng + g) * spg:(2 * ng + g + 1) * spg], tp_rows).reshape(bp, seq, hh, hd)
                kv_p[g].append(jnp.stack([kk[:, seq - nl:], vv[:, seq - nl:]], axis=2))

            slots_s = _proj_in(xs, g1, ms[0][None], ms[1][None], w_in, sr, sr)
            o_s = _dil_sample_attn(slots_s, [c[j] for c in caches], slope_tab)
            xs = _proj_out(pad_rows(o_s.reshape(bs, hh * hd), sr), w_out, xs, ms[2][None], sr, sr)
            for g in range(ng):
                kn = head_major(slots_s[(ng + g) * spg:(ng + g + 1) * spg], sr)[:bs]
                vn = head_major(slots_s[(2 * ng + g) * spg:(2 * ng + g + 1) * spg], sr)[:bs]
                new = jnp.stack([kn, vn], axis=1)[:, None]
                kv_s[g].append(jnp.concatenate([caches[g][j][:, 1:], new], axis=1))
        else:
            w_in = ret_w_in[j].astype(BF16)
            w_out = ret_w_out[j].astype(BF16)
            gn = ret_gn_g[j].reshape(1, RET_HEADS * ret_dv)
            slots = _proj_in(xp, g1, mp[0], mp[1], w_in, 1024, seq)
            y, st_p = _ret_prompt(slots, ret_tabs, gn, bp, seq, ret_dk, ret_dv)
            xp = _proj_out(y, w_out, xp, mp[2], 1024, seq)
            ret_p.append(st_p)
            slots_s = _proj_in(xs, g1, ms[0][None], ms[1][None], w_in, sr, sr)
            y_s, st_s = _ret_sample(slots_s, state_ret[j], ret_tabs, gn, ret_dk, ret_dv)
            xs = _proj_out(pad_rows(y_s.reshape(bs, RET_HEADS * ret_dv), sr), w_out, xs, ms[2][None], sr, sr)
            ret_s.append(st_s)

        last = i == depth - 1
        wq_t = wq_t_all[i]
        xp = _peer_layer(xp, g2, mp[3], mp[4], mp[5], wq_t, peer_keys[i], u_bf, v_bf, i, final_g2, cand,
                         tp=256, td=512, et=512, rows_per_mod=seq, final_norm=last)
        padr = lambda a: pad_rows(a, PEER_SAMPLE_ROWS)
        xs_new = _peer_layer(padr(xs), g2, padr(ms[3])[None], padr(ms[4])[None], padr(ms[5])[None],
                             wq_t, peer_keys[i], u_bf, v_bf, i, final_g2, cand,
                             tp=PEER_SAMPLE_ROWS, td=PEER_SAMPLE_ROWS, et=512,
                             rows_per_mod=PEER_SAMPLE_ROWS, final_norm=last)
        xs = xs_new[:sr]

    y_prompt = xp.reshape(bp, seq, d)
    y_sample = xs[:bs].reshape(bs, 1, d)
    return (y_prompt, y_sample,
            jnp.stack(pool_p), jnp.stack(pool_s),
            jnp.stack(kv_p[0]), jnp.stack(kv_s[0]),
            jnp.stack(kv_p[1]), jnp.stack(kv_s[1]),
            jnp.stack(kv_p[2]), jnp.stack(kv_s[2]),
            jnp.stack(ret_p), jnp.stack(ret_s))
```
